```python
import jax, jax.numpy as jnp
from jax import lax
import numpy as np

D_MODEL = 1024
BATCH = 4
SEQ = 4096
DEPTH = 4

RWKV_HEADS = 4
RWKV_HEAD_DIM = 64
RWKV_WIDTH = RWKV_HEADS * RWKV_HEAD_DIM
DECAY_LORA = 64
AAA_LORA = 64
GATE_LORA = 128
RWKV_LN_EPS = 64e-5

FOX_HEADS = 4
FOX_HEAD_DIM = 64
FOX_WIDTH = FOX_HEADS * FOX_HEAD_DIM

MLA_HEADS = 4
QK_NOPE_DIM = 128
QK_ROPE_DIM = 64
V_HEAD_DIM = 128
Q_LORA_RANK = 384
KV_LORA_RANK = 256
MLA_WIDTH = MLA_HEADS * V_HEAD_DIM
ROPE_THETA = 10000.0

N_BRANCHES = 3
D_FF = 4 * D_MODEL
Q_BLOCK = 128
NORM_EPS = 1e-6
MASK_VALUE = -1e30

RWKV_COLS = 3 * RWKV_WIDTH + DECAY_LORA + AAA_LORA + GATE_LORA
FOX_COLS = 3 * FOX_WIDTH + FOX_HEADS
MLA_COLS = Q_LORA_RANK + KV_LORA_RANK + QK_ROPE_DIM
GATE_COLS = N_BRANCHES * D_MODEL
IN_COLS = RWKV_COLS + FOX_COLS + MLA_COLS + GATE_COLS

kernel_name = 'hybrid_rwkv7_fox_mla_gated_decoder'


def rms_norm(x, gain, eps=NORM_EPS):
    xf = x.astype(jnp.float32)
    y = xf * lax.rsqrt(jnp.mean(xf * xf, axis=-1, keepdims=True) + eps)
    return y.astype(x.dtype) * gain


def token_shift(z):
    return jnp.pad(z, ((0, 0), (1, 0), (0, 0)))[:, :-1]


def apply_rope(x, cos, sin):
    x1, x2 = jnp.split(x, 2, axis=-1)
    return jnp.concatenate([x1 * cos - x2 * sin, x1 * sin + x2 * cos], axis=-1).astype(x.dtype)


def causal_block_attention(q, k, v, scale, log_forget_cum=None):
    B, S, H, _ = q.shape
    k_pos = jnp.arange(S)
    cum_bhs = None if log_forget_cum is None else jnp.transpose(log_forget_cum, (0, 2, 1))

    def one_block(i):
        start = i * Q_BLOCK
        qb = lax.dynamic_slice_in_dim(q, start, Q_BLOCK, axis=1)
        s = jnp.einsum('bqhd,bkhd->bhqk', qb, k, preferred_element_type=jnp.float32) * scale
        if cum_bhs is not None:
            cq = lax.dynamic_slice_in_dim(cum_bhs, start, Q_BLOCK, axis=2)
            s = s + cq[..., :, None] - cum_bhs[..., None, :]
        q_pos = start + jnp.arange(Q_BLOCK)
        s = jnp.where(k_pos[None, :] <= q_pos[:, None], s, MASK_VALUE)
        p = jax.nn.softmax(s, axis=-1).astype(v.dtype)
        return jnp.einsum('bhqk,bkhd->bqhd', p, v)

    out = lax.map(one_block, jnp.arange(S // Q_BLOCK))
    return jnp.moveaxis(out, 0, 1).reshape(B, S, H, v.shape[-1])


def rwkv7_mix(z, w0, w_decay_up, a0, w_aaa_up, w_gate_up, k_k, k_a, r_k, ln_g, ln_b):
    B, S, _ = z.shape
    offs = np.cumsum([RWKV_WIDTH, RWKV_WIDTH, RWKV_WIDTH, DECAY_LORA, AAA_LORA]).tolist()
    r, k, v, wd, ad, gd = jnp.split(z, offs, axis=-1)
    w_log = -jax.nn.softplus(-(w0 + jnp.tanh(wd) @ w_decay_up)) - 0.5
    decay = jnp.exp(-jnp.exp(w_log.astype(jnp.float32)))
    a = jax.nn.sigmoid(a0 + ad @ w_aaa_up)
    g = jax.nn.sigmoid(gd) @ w_gate_up
    kk = k * k_k
    k = k * (1 + (a - 1) * k_a)
    heads = lambda t: t.reshape(B, S, RWKV_HEADS, RWKV_HEAD_DIM).astype(jnp.float32)
    r_h, k_h, v_h, a_h, w_h, kk_h = map(heads, (r, k, v, a, decay, kk))
    kk_h = kk_h / jnp.maximum(jnp.linalg.norm(kk_h, axis=-1, keepdims=True), 1e-12)

    def step(state, inp):
        r_t, w_t, k_t, v_t, kk_t, a_t = inp
        sa = jnp.einsum('bhvk,bhk->bhv', state, -kk_t)
        state = (state * w_t[:, :, None, :] + sa[..., None] * (kk_t * a_t)[:, :, None, :]
                 + v_t[..., None] * k_t[:, :, None, :])
        return state, jnp.einsum('bhvk,bhk->bhv', state, r_t)

    state0 = jnp.zeros((B, RWKV_HEADS, RWKV_HEAD_DIM, RWKV_HEAD_DIM), jnp.float32)
    xs = tuple(jnp.moveaxis(t, 1, 0) for t in (r_h, w_h, k_h, v_h, kk_h, a_h))
    _, y = lax.scan(step, state0, xs)
    y = jnp.moveaxis(y, 0, 1)
    mean = jnp.mean(y, -1, keepdims=True)
    var = jnp.mean(jnp.square(y - mean), -1, keepdims=True)
    y = ((y - mean) * lax.rsqrt(var + RWKV_LN_EPS)).reshape(B, S, RWKV_WIDTH) * ln_g + ln_b
    bonus = jnp.sum(r_h * k_h * r_k, -1, keepdims=True) * v_h
    y = (y + bonus.reshape(B, S, RWKV_WIDTH)) * g
    return y.astype(z.dtype)


def fox_mix(z, b_forget):
    B, S, _ = z.shape
    q, k, v, f = jnp.split(z, [FOX_WIDTH, 2 * FOX_WIDTH, 3 * FOX_WIDTH], axis=-1)
    heads = lambda t: t.reshape(B, S, FOX_HEADS, FOX_HEAD_DIM)
    log_f = jax.nn.log_sigmoid((f + b_forget).astype(jnp.float32))
    cum = jnp.cumsum(log_f, axis=1)
    o = causal_block_attention(heads(q), heads(k), heads(v), FOX_HEAD_DIM ** -0.5, cum)
    return o.reshape(B, S, FOX_WIDTH)


def mla_mix(z, q_norm_g, w_q_up, kv_norm_g, w_kv_up, cos, sin):
    B, S, _ = z.shape
    q_lat, kv_lat, k_pe = jnp.split(z, [Q_LORA_RANK, Q_LORA_RANK + KV_LORA_RANK], axis=-1)
    q = (rms_norm(q_lat, q_norm_g) @ w_q_up).reshape(B, S, MLA_HEADS, QK_NOPE_DIM + QK_ROPE_DIM)
    kv = (rms_norm(kv_lat, kv_norm_g) @ w_kv_up).reshape(B, S, MLA_HEADS, QK_NOPE_DIM + V_HEAD_DIM)
    q_nope, q_pe = jnp.split(q, [QK_NOPE_DIM], axis=-1)
    k_nope, v = jnp.split(kv, [QK_NOPE_DIM], axis=-1)
    q_pe = apply_rope(q_pe, cos[:, :, None, :], sin[:, :, None, :])
    k_pe = apply_rope(k_pe, cos, sin)[:, :, None, :]
    q_full = jnp.concatenate([q_nope, q_pe], axis=-1)
    k_full = jnp.concatenate([k_nope, jnp.broadcast_to(k_pe, (B, S, MLA_HEADS, QK_ROPE_DIM))], axis=-1)
    o = causal_block_attention(q_full, k_full, v, (QK_NOPE_DIM + QK_ROPE_DIM) ** -0.5)
    return o.reshape(B, S, MLA_WIDTH)


def setup_inputs(seed: int = 0) -> dict:
    key = jax.random.key(seed)
    ks = iter(jax.random.split(key, 40))
    L, D = DEPTH, D_MODEL
    nrm = lambda shape, scale: jax.random.normal(next(ks), shape, jnp.float32) * scale
    gain = lambda shape: 1.0 + nrm(shape, 0.02)
    x = nrm((BATCH, SEQ, D), 1.0)
    c = nrm((BATCH, D), 1.0)
    positions = (jnp.arange(SEQ, dtype=jnp.int32)[None, :]
                 + jax.random.randint(next(ks), (BATCH, 1), 0, 1024, dtype=jnp.int32))
    return {
        'x': x, 'c': c, 'positions': positions,
        'w_in': nrm((L, D, IN_COLS), D ** -0.5),
        'mu_shift': jax.random.uniform(next(ks), (L, RWKV_COLS), jnp.float32),
        'w0': -1.0 + nrm((L, RWKV_WIDTH), 0.5),
        'w_decay_up': nrm((L, DECAY_LORA, RWKV_WIDTH), DECAY_LORA ** -0.5),
        'a0': nrm((L, RWKV_WIDTH), 0.1),
        'w_aaa_up': nrm((L, AAA_LORA, RWKV_WIDTH), AAA_LORA ** -0.5),
        'w_gate_up': nrm((L, GATE_LORA, RWKV_WIDTH), GATE_LORA ** -0.5),
        'k_k': 0.85 + nrm((L, RWKV_WIDTH), 0.05),
        'k_a': gain((L, RWKV_WIDTH)),
        'r_k': nrm((L, RWKV_HEADS, RWKV_HEAD_DIM), 0.1),
        'ln_x_g': gain((L, RWKV_WIDTH)),
        'ln_x_b': nrm((L, RWKV_WIDTH), 0.02),
        'b_forget': 2.0 + nrm((L, FOX_HEADS), 0.5),
        'q_norm_g': gain((L, Q_LORA_RANK)),
        'w_q_up': nrm((L, Q_LORA_RANK, MLA_HEADS * (QK_NOPE_DIM + QK_ROPE_DIM)), Q_LORA_RANK ** -0.5),
        'kv_norm_g': gain((L, KV_LORA_RANK)),
        'w_kv_up': nrm((L, KV_LORA_RANK, MLA_HEADS * (QK_NOPE_DIM + V_HEAD_DIM)), KV_LORA_RANK ** -0.5),
        'w_branch_a': nrm((L, RWKV_WIDTH, D), RWKV_WIDTH ** -0.5),
        'w_branch_b': nrm((L, FOX_WIDTH, D), FOX_WIDTH ** -0.5),
        'w_branch_c': nrm((L, MLA_WIDTH, D), MLA_WIDTH ** -0.5),
        'w_out': nrm((L, D, D), D ** -0.5),
        'w_mod': nrm((L, D, 6 * D), 0.5 * D ** -0.5),
        'b_mod': nrm((L, 6 * D), 0.02),
        'norm_mix_pre': gain((L, D)),
        'norm_mix_post': gain((L, D)),
        'norm_ffn_pre': gain((L, D)),
        'norm_ffn_post': gain((L, D)),
        'w_ffn_up': nrm((L, D, D_FF), D ** -0.5),
        'w_ffn_down': nrm((L, D_FF, D), D_FF ** -0.5),
    }


def reference(x, c, positions, w_in, mu_shift, w0, w_decay_up, a0, w_aaa_up, w_gate_up, k_k, k_a, r_k,
              ln_x_g, ln_x_b, b_forget, q_norm_g, w_q_up, kv_norm_g, w_kv_up, w_branch_a, w_branch_b,
              w_branch_c, w_out, w_mod, b_mod, norm_mix_pre, norm_mix_post, norm_ffn_pre, norm_ffn_post,
              w_ffn_up, w_ffn_down):
    B, S, D = x.shape
    inv_freq = ROPE_THETA ** (-jnp.arange(0, QK_ROPE_DIM, 2, dtype=jnp.float32) / QK_ROPE_DIM)
    ang = positions.astype(jnp.float32)[..., None] * inv_freq
    cos, sin = jnp.cos(ang), jnp.sin(ang)
    c_act = jax.nn.silu(c)
    split_cols = [RWKV_COLS, RWKV_COLS + FOX_COLS, RWKV_COLS + FOX_COLS + MLA_COLS]
    for l in range(DEPTH):
        mod = c_act @ w_mod[l] + b_mod[l]
        sh_m, sc_m, g_m, sh_f, sc_f, g_f = [m[:, None, :] for m in jnp.split(mod, 6, axis=-1)]

        h = rms_norm(x, norm_mix_pre[l]) * (1 + sc_m) + sh_m
        z = h @ w_in[l]
        z_a, z_b, z_c, z_g = jnp.split(z, split_cols, axis=-1)
        z_a = z_a + (token_shift(z_a) - z_a) * mu_shift[l]
        y_a = rwkv7_mix(z_a, w0[l], w_decay_up[l], a0[l], w_aaa_up[l], w_gate_up[l], k_k[l], k_a[l],
                        r_k[l], ln_x_g[l], ln_x_b[l])
        y_b = fox_mix(z_b, b_forget[l])
        y_c = mla_mix(z_c, q_norm_g[l], w_q_up[l], kv_norm_g[l], w_kv_up[l], cos, sin)
        gates = jax.nn.sigmoid(z_g).reshape(B, S, N_BRANCHES, D)
        merged = (gates[:, :, 0] * (y_a @ w_branch_a[l]) + gates[:, :, 1] * (y_b @ w_branch_b[l])
                  + gates[:, :, 2] * (y_c @ w_branch_c[l]))
        x = x + g_m * rms_norm(merged @ w_out[l], norm_mix_post[l])

        h = rms_norm(x, norm_ffn_pre[l]) * (1 + sc_f) + sh_f
        u = jnp.square(jax.nn.relu(h @ w_ffn_up[l]))
        x = x + g_f * rms_norm(u @ w_ffn_down[l], norm_ffn_post[l])
    return x
```

```python
import functools

import jax
import jax.numpy as jnp
from jax import lax
from jax.experimental import pallas as pl
from jax.experimental.pallas import tpu as pltpu

F32 = jnp.float32
BF16 = jnp.bfloat16
HIGHEST = lax.Precision.HIGHEST

D_MODEL = 1024
N_HEADS = 4
HEAD64 = 64
RWKV_WIDTH = N_HEADS * HEAD64
QK_NOPE_DIM = 128
QK_ROPE_DIM = 64
V_HEAD_DIM = 128
Q_LORA_RANK = 384
KV_LORA_RANK = 256
ROPE_THETA = 10000.0
D_FF = 4 * D_MODEL
NORM_EPS = 1e-6
RWKV_LN_EPS = 64e-5
MASK_VALUE = -1e30
CHUNK = 64

_FOX0 = 1024
_MLA0 = _FOX0 + 3 * 256 + N_HEADS
_GATE0 = _MLA0 + Q_LORA_RANK + KV_LORA_RANK + QK_ROPE_DIM

_C_RWKV = 0
_C_FQ = 1024
_C_FK = 1536
_C_FV = 2048
_C_FF = 2560
_C_QL = 2688
_C_KVL = 3072
_C_KPE = 3328
_C_END = 3456

VMEM_LIMIT = 56 * 1024 * 1024


def _bdot(a, b):
    return jnp.dot(a.astype(BF16), b.astype(BF16), preferred_element_type=F32)


def _bdot_nt(a, b):
    return lax.dot_general(a.astype(BF16), b.astype(BF16), (((1,), (1,)), ((), ())),
                           preferred_element_type=F32)


def _bdot_tn(a, b):
    return lax.dot_general(a.astype(BF16), b.astype(BF16), (((0,), (0,)), ((), ())),
                           preferred_element_type=F32)


def _fdot(a, b):
    return jnp.dot(a, b, preferred_element_type=F32, precision=HIGHEST)


def _split2_dot(a, b_bf16):
    hi = a.astype(BF16)
    lo = (a - hi.astype(F32)).astype(BF16)
    return (jnp.dot(hi, b_bf16, preferred_element_type=F32)
            + jnp.dot(lo, b_bf16, preferred_element_type=F32))


def _split3(a):
    hi = a.astype(BF16)
    r1 = a - hi.astype(F32)
    mid = r1.astype(BF16)
    lo = (r1 - mid.astype(F32)).astype(BF16)
    return hi, mid, lo


def _rms(x, gain):
    return x * lax.rsqrt(jnp.mean(x * x, axis=-1, keepdims=True) + NORM_EPS) * gain


def _softplus(y):
    return jnp.maximum(y, 0.0) + jnp.log(1.0 + jnp.exp(-jnp.abs(y)))


def _sigmoid(y):
    return 1.0 / (1.0 + jnp.exp(-y))


def _iota(shape, dim):
    return lax.broadcasted_iota(jnp.int32, shape, dim)


def _full(shape):
    return pl.BlockSpec(shape, lambda *_: (0,) * len(shape))


def _params(sem):
    return pltpu.CompilerParams(dimension_semantics=sem, vmem_limit_bytes=VMEM_LIMIT)


def _mod_kernel(c_ref, w_ref, b_ref, o_ref):
    c = c_ref[...]
    o_ref[...] = _fdot(c * _sigmoid(c), w_ref[...]) + b_ref[...]


def _modulation(c8, w_mod, b_mod):
    L, D, D6 = w_mod.shape
    return pl.pallas_call(
        _mod_kernel,
        grid=(L, D6 // D),
        in_specs=[pl.BlockSpec((8, D), lambda l, j: (0, 0)),
                  pl.BlockSpec((None, D, D), lambda l, j: (l, 0, j)),
                  pl.BlockSpec((None, 1, D), lambda l, j: (l, 0, j))],
        out_specs=pl.BlockSpec((None, 8, D), lambda l, j: (l, 0, j)),
        out_shape=jax.ShapeDtypeStruct((L, 8, D6), F32),
        compiler_params=_params(("parallel", "parallel")),
        name="adaln_mod",
    )(c8, w_mod, b_mod.reshape(L, 1, D6))


def _rope_kernel(pos_ref, inv_ref, cos_ref, sin_ref):
    ang = pos_ref[...] * inv_ref[...]
    lane = _iota(ang.shape, 1)
    cos_ref[...] = jnp.where(lane < 64, jnp.cos(ang), 0.0)
    s = jnp.sin(ang)
    sin_ref[...] = jnp.where(lane < 32, -s, jnp.where(lane < 64, s, 0.0))


def _rope_tables(positions, tm):
    B, S = positions.shape
    inv_freq = ROPE_THETA ** (-jnp.arange(0, QK_ROPE_DIM, 2, dtype=F32) / QK_ROPE_DIM)
    inv128 = jnp.tile(inv_freq, 4).reshape(1, 128)
    pos = positions.astype(F32).reshape(B, S, 1)
    out = jax.ShapeDtypeStruct((B, S, 128), F32)
    return pl.pallas_call(
        _rope_kernel,
        grid=(B, S // tm),
        in_specs=[pl.BlockSpec((None, tm, 1), lambda b, i: (b, i, 0)), _full((1, 128))],
        out_specs=[pl.BlockSpec((None, tm, 128), lambda b, i: (b, i, 0))] * 2,
        out_shape=[out, out],
        compiler_params=_params(("parallel", "parallel")),
        name="rope_tables",
    )(pos, inv128)


def _rope128(x, cos_t, sin_t):
    lane = _iota(x.shape, 1)
    partner = jnp.where(lane < 32, pltpu.roll(x, 96, 1), pltpu.roll(x, 32, 1))
    return x * cos_t + partner * sin_t


def _inproj_kernel(x_ref, mod_ref, gpre_ref, w1_ref, bf_ref, selq_ref, selk_ref, cq_ref, ck_ref,
                   qg_ref, wq_ref, kvg_ref, wkv_ref, cos_ref, sin_ref,
                   za_ref, fq_ref, fk_ref, fv_ref, mq_ref, mk_ref, mv_ref, carry_ref):
    i = pl.program_id(1)
    tm = x_ref.shape[0]

    @pl.when(i == 0)
    def _():
        carry_ref[...] = jnp.zeros_like(carry_ref)

    x = x_ref[...]
    h = (_rms(x, gpre_ref[...]) * (1.0 + mod_ref[1:2, :]) + mod_ref[0:1, :]).astype(BF16)

    za_ref[...] = jnp.dot(h, w1_ref[:, _C_RWKV:_C_FQ], preferred_element_type=F32)

    f = jnp.dot(h, w1_ref[:, _C_FF:_C_QL], preferred_element_type=F32) + bf_ref[...]
    logf = -_softplus(-f)
    tri = (_iota((tm, tm), 0) >= _iota((tm, tm), 1)).astype(BF16)
    l_hi, l_mid, l_lo = _split3(logf)
    cum = (jnp.dot(tri, l_hi, preferred_element_type=F32)
           + jnp.dot(tri, l_mid, preferred_element_type=F32)
           + jnp.dot(tri, l_lo, preferred_element_type=F32)) + carry_ref[0:1, :]
    carry_ref[0:1, :] = cum[tm - 1:tm, :]
    cc = jnp.concatenate(_split3(cum), axis=1)
    fq = (jnp.dot(h, w1_ref[:, _C_FQ:_C_FK], preferred_element_type=F32)
          + jnp.dot(cc, selq_ref[...], preferred_element_type=F32) + cq_ref[...])
    fk = (jnp.dot(h, w1_ref[:, _C_FK:_C_FV], preferred_element_type=F32)
          + jnp.dot(cc, selk_ref[...], preferred_element_type=F32) + ck_ref[...])
    fv = jnp.dot(h, w1_ref[:, _C_FV:_C_FF], preferred_element_type=F32)
    for hd in range(N_HEADS):
        sl = slice(128 * hd, 128 * hd + 128)
        fq_ref[hd] = fq[:, sl].astype(BF16)
        fk_ref[hd] = fk[:, sl].astype(BF16)
        fv_ref[hd] = fv[:, sl].astype(BF16)

    cos_t = cos_ref[...]
    sin_t = sin_ref[...]
    q_lat = jnp.dot(h, w1_ref[:, _C_QL:_C_KVL], preferred_element_type=F32)
    q = _bdot(_rms(q_lat, qg_ref[...]), wq_ref[...])
    kv_lat = jnp.dot(h, w1_ref[:, _C_KVL:_C_KPE], preferred_element_type=F32)
    kv = _bdot(_rms(kv_lat, kvg_ref[...]), wkv_ref[...])
    k_pe = _rope128(jnp.dot(h, w1_ref[:, _C_KPE:_C_END], preferred_element_type=F32), cos_t, sin_t)
    k_pe = k_pe.astype(BF16)
    for hd in range(N_HEADS):
        o = 256 * hd
        q_pe = _rope128(q[:, o + 128:o + 256], cos_t, sin_t)
        mq_ref[hd] = jnp.concatenate([q[:, o:o + 128], q_pe], axis=1).astype(BF16)
        mk_ref[hd] = jnp.concatenate([kv[:, o:o + 128].astype(BF16), k_pe], axis=1)
        mv_ref[hd] = kv[:, o + 128:o + 256].astype(BF16)


def _inproj(x, mod_l, gpre, w1, bf, selq, selk, cq, ck, qg, wq, kvg, wkv, cos_t, sin_t, tm):
    B, S, D = x.shape
    H = N_HEADS
    tile = lambda w: pl.BlockSpec((None, tm, w), lambda b, i: (b, i, 0))
    head = lambda w: pl.BlockSpec((None, H, tm, w), lambda b, i: (b, 0, i, 0))
    hs = lambda w: jax.ShapeDtypeStruct((B, H, S, w), BF16)
    return pl.pallas_call(
        _inproj_kernel,
        grid=(B, S // tm),
        in_specs=[tile(D), pl.BlockSpec((None, 6, D), lambda b, i: (b, 0, 0)), _full(gpre.shape),
                  _full(w1.shape), _full(bf.shape), _full(selq.shape), _full(selk.shape),
                  _full(cq.shape), _full(ck.shape), _full(qg.shape), _full(wq.shape),
                  _full(kvg.shape), _full(wkv.shape), tile(128), tile(128)],
        out_specs=[tile(1024), head(128), head(128), head(128), head(256), head(256), head(128)],
        out_shape=[jax.ShapeDtypeStruct((B, S, 1024), F32), hs(128), hs(128), hs(128),
                   hs(256), hs(256), hs(128)],
        scratch_shapes=[pltpu.VMEM((8, 128), F32)],
        compiler_params=_params(("parallel", "arbitrary")),
        name="in_proj",
    )(x, mod_l, gpre, w1, bf, selq, selk, cq, ck, qg, wq, kvg, wkv, cos_t, sin_t)


def _rwkv_kernel(z_ref, mu_ref, w0_ref, wd_ref, a0_ref, wa_ref, wg_ref, kk_ref, ka_ref, rk_ref,
                 lng_ref, lnb_ref, y_ref,
                 prev_ref, h_ref, rt_ref, at_ref, bt_ref, kt_ref, bh_ref, kh_ref, v_ref, gam_ref,
                 ys_ref):
    i = pl.program_id(1)
    tm = z_ref.shape[0]
    W = RWKV_WIDTH
    C = CHUNK

    @pl.when(i == 0)
    def _():
        prev_ref[...] = jnp.zeros_like(prev_ref)
        h_ref[...] = jnp.zeros_like(h_ref)

    z = z_ref[...]
    zprev = jnp.where(_iota((tm, 1), 0) == 0, prev_ref[0:1, :], pltpu.roll(z, 1, 0))
    prev_ref[0:1, :] = z[tm - 1:tm, :]
    zm = z + (zprev - z) * mu_ref[...]
    r = zm[:, 0:W]
    k = zm[:, W:2 * W]
    v = zm[:, 2 * W:3 * W]
    lora_in = zm[:, 3 * W:3 * W + 128]
    gd = zm[:, 3 * W + 128:3 * W + 256]
    w_log = -_softplus(-(w0_ref[...] + _fdot(jnp.tanh(lora_in), wd_ref[...]))) - 0.5
    lw = -jnp.exp(w_log)
    a = _sigmoid(a0_ref[...] + _fdot(lora_in, wa_ref[...]))
    g = _fdot(_sigmoid(gd), wg_ref[...])
    seg = ((_iota((W, W), 0) // HEAD64) == (_iota((W, W), 1) // HEAD64)).astype(BF16)
    kk = k * kk_ref[...]
    kk = kk / jnp.maximum(jnp.sqrt(_split2_dot(kk * kk, seg)), 1e-12)
    k2 = k * (1.0 + (a - 1.0) * ka_ref[...])

    tri = (_iota((C, C), 0) >= _iota((C, C), 1)).astype(F32)
    cls, ces = [], []
    for c in range(tm // C):
        cl_c = _fdot(tri, lw[c * C:(c + 1) * C, :])
        cls.append(cl_c)
        ces.append(jnp.broadcast_to(cl_c[C - 1:C, :], (C, W)))
    cl = jnp.concatenate(cls, axis=0)
    ce = jnp.concatenate(ces, axis=0)
    e_neg = jnp.exp(-cl)
    e_end = jnp.exp(ce - cl)
    rt_ref[...] = r * jnp.exp(cl)
    at_ref[...] = -kk * jnp.exp(cl - lw)
    bt_ref[...] = kk * a * e_neg
    kt_ref[...] = k2 * e_neg
    bh_ref[...] = kk * a * e_end
    kh_ref[...] = k2 * e_end
    v_ref[...] = v
    gam_ref[...] = jnp.exp(ce)

    lane_head = _iota((1, W), 1) // HEAD64
    row_t = _iota((W, W), 0) % C
    col_t = _iota((W, W), 1) % C
    strict_lower = row_t > col_t
    incl_lower = row_t >= col_t
    eye_mask = _iota((W, W), 0) == _iota((W, W), 1)
    eye = eye_mask.astype(F32)

    def stack(ref, rows):
        xc = ref[rows, :]
        return jnp.concatenate([jnp.where(lane_head == hd, xc, 0.0) for hd in range(N_HEADS)],
                               axis=0).astype(BF16)

    def chunk_step(c, hst):
        rows = pl.ds(pl.multiple_of(c * C, C), C)
        a_s, r_s, b_s, k_s = stack(at_ref, rows), stack(rt_ref, rows), stack(bt_ref, rows), stack(kt_ref, rows)
        v_s, bh_s, kh_s = stack(v_ref, rows), stack(bh_ref, rows), stack(kh_ref, rows)
        m = _bdot_nt(jnp.concatenate([a_s, r_s], axis=0), jnp.concatenate([b_s, k_s], axis=0))
        a_ab = jnp.where(strict_lower, m[:W, :W], 0.0)
        a_ak = jnp.where(strict_lower, m[:W, W:], 0.0)
        m_rb = jnp.where(incl_lower, m[W:, :W], 0.0)
        m_rk = jnp.where(incl_lower, m[W:, W:], 0.0)
        av = _bdot(jnp.concatenate([a_ak, m_rk], axis=0), v_s)
        t_inv = eye + a_ab
        p = a_ab
        for _ in range(5):
            p = _bdot(p, p)
            t_inv = t_inv + _bdot(t_inv, p)
        x = _bdot(t_inv, jnp.concatenate([a_s.astype(F32), av[:W]], axis=1))
        ry = _bdot(m_rb, x)
        r_p = r_s.astype(F32) + ry[:, :W]
        y_p = av[W:] + ry[:, W:]
        rhs = jnp.concatenate(
            [x.astype(BF16), jnp.concatenate([jnp.zeros((W, W), BF16), v_s], axis=1)], axis=0)
        pg = _bdot_tn(jnp.concatenate([bh_s, kh_s], axis=0), rhs)
        hb = hst.astype(BF16)
        ys = _bdot(r_p, hb) + y_p
        ys_ref[rows, :] = ys[0:C] + ys[C:2 * C] + ys[2 * C:3 * C] + ys[3 * C:4 * C]
        gam_row = gam_ref[pl.ds(pl.multiple_of(c * C, C), 1), :]
        gam_col = jnp.sum(jnp.where(eye_mask, gam_row, 0.0), axis=1, keepdims=True)
        return gam_col * hst + _bdot(pg[:, :W], hb) + pg[:, W:]

    h_ref[...] = lax.fori_loop(0, tm // C, chunk_step, h_ref[...])

    y = ys_ref[...]
    inv = 1.0 / HEAD64
    mean = _split2_dot(y, seg) * inv
    d = y - mean
    var = _split2_dot(d * d, seg) * inv
    yn = d * lax.rsqrt(var + RWKV_LN_EPS) * lng_ref[...] + lnb_ref[...]
    bonus = _split2_dot(r * k2 * rk_ref[...], seg) * v
    y_ref[...] = ((yn + bonus) * g).astype(BF16)


def _rwkv(za, mu, w0, wd, a0, wa, wg, k_k, k_a, r_k, ln_g, ln_b, tm):
    B, S, _ = za.shape
    W = RWKV_WIDTH
    smalls = [mu, w0, wd, a0, wa, wg, k_k, k_a, r_k, ln_g, ln_b]
    return pl.pallas_call(
        _rwkv_kernel,
        grid=(B, S // tm),
        in_specs=[pl.BlockSpec((None, tm, 1024), lambda b, i: (b, i, 0))] + [_full(s.shape) for s in smalls],
        out_specs=pl.BlockSpec((None, tm, W), lambda b, i: (b, i, 0)),
        out_shape=jax.ShapeDtypeStruct((B, S, W), BF16),
        scratch_shapes=[pltpu.VMEM((8, 1024), F32), pltpu.VMEM((W, W), F32)]
                       + [pltpu.VMEM((tm, W), F32)] * 9,
        compiler_params=_params(("parallel", "arbitrary")),
        name="rwkv7_chunked",
    )(za, *smalls)


def _attn_kernel(q_ref, k_ref, v_ref, o_ref):
    qi = pl.program_id(1)
    H, tq, _ = q_ref.shape
    dv = v_ref.shape[-1]
    causal = _iota((tq, tq), 0) >= _iota((tq, tq), 1)

    def update(carry, s, vs):
        m, l, acc = carry
        m_new = jnp.maximum(m, jnp.max(s, axis=1, keepdims=True))
        p = jnp.exp(s - m_new)
        alpha = jnp.exp(m - m_new)
        l_new = alpha * l + jnp.sum(p, axis=1, keepdims=True)
        acc_new = alpha * acc + jnp.dot(p.astype(BF16), vs, preferred_element_type=F32)
        return m_new, l_new, acc_new

    for hd in range(H):
        q = q_ref[hd]

        def body(j, carry):
            rows = pl.ds(pl.multiple_of(j * tq, tq), tq)
            return update(carry, _bdot_nt(q, k_ref[hd, rows, :]), v_ref[hd, rows, :])

        init = (jnp.full((tq, 1), MASK_VALUE, F32), jnp.zeros((tq, 1), F32), jnp.zeros((tq, dv), F32))
        carry = lax.fori_loop(0, qi, body, init)
        rows = pl.ds(pl.multiple_of(qi * tq, tq), tq)
        s = jnp.where(causal, _bdot_nt(q, k_ref[hd, rows, :]), MASK_VALUE)
        _, l, acc = update(carry, s, v_ref[hd, rows, :])
        o_ref[:, hd * dv:(hd + 1) * dv] = (acc / l).astype(BF16)


def _attention(q, k, v, tq, name):
    B, H, S, dq = q.shape
    dv = v.shape[-1]
    return pl.pallas_call(
        _attn_kernel,
        grid=(B, S // tq),
        in_specs=[pl.BlockSpec((None, H, tq, dq), lambda b, i: (b, 0, i, 0)),
                  pl.BlockSpec((None, H, S, dq), lambda b, i: (b, 0, 0, 0)),
                  pl.BlockSpec((None, H, S, dv), lambda b, i: (b, 0, 0, 0))],
        out_specs=pl.BlockSpec((None, tq, H * dv), lambda b, i: (b, i, 0)),
        out_shape=jax.ShapeDtypeStruct((B, S, H * dv), BF16),
        compiler_params=_params(("parallel", "arbitrary")),
        name=name,
    )(q, k, v)


def _merge_kernel(x_ref, mod_ref, gpre_ref, gpost_ref, wg_ref, ya_ref, yb_ref, yc_ref,
                  wa_ref, wb_ref, wc_ref, wo_ref, o_ref):
    D = D_MODEL
    x = x_ref[...]
    h = (_rms(x, gpre_ref[...]) * (1.0 + mod_ref[1:2, :]) + mod_ref[0:1, :]).astype(BF16)
    merged = None
    for j, (y_ref, w_ref) in enumerate(((ya_ref, wa_ref), (yb_ref, wb_ref), (yc_ref, wc_ref))):
        gate = _sigmoid(jnp.dot(h, wg_ref[:, j * D:(j + 1) * D], preferred_element_type=F32))
        term = gate * jnp.dot(y_ref[...], w_ref[...], preferred_element_type=F32)
        merged = term if merged is None else merged + term
    out = _bdot(merged, wo_ref[...])
    o_ref[...] = x + mod_ref[2:3, :] * _rms(out, gpost_ref[...])


def _merge(x, mod_l, gpre, gpost, wg, ya, yb, yc, wa, wb, wc, wo, tm):
    B, S, D = x.shape
    tile = lambda w: pl.BlockSpec((None, tm, w), lambda b, i: (b, i, 0))
    return pl.pallas_call(
        _merge_kernel,
        grid=(B, S // tm),
        in_specs=[tile(D), pl.BlockSpec((None, 6, D), lambda b, i: (b, 0, 0)), _full(gpre.shape),
                  _full(gpost.shape), _full(wg.shape), tile(ya.shape[-1]), tile(yb.shape[-1]),
                  tile(yc.shape[-1]), _full(wa.shape), _full(wb.shape), _full(wc.shape),
                  _full(wo.shape)],
        out_specs=tile(D),
        out_shape=jax.ShapeDtypeStruct((B, S, D), F32),
        compiler_params=_params(("parallel", "parallel")),
        name="merge_out",
    )(x, mod_l, gpre, gpost, wg, ya, yb, yc, wa, wb, wc, wo)


def _ffn_kernel(x_ref, mod_ref, gpre_ref, gpost_ref, wu_ref, wd_ref, o_ref):
    D = D_MODEL
    x = x_ref[...]
    h = (_rms(x, gpre_ref[...]) * (1.0 + mod_ref[4:5, :]) + mod_ref[3:4, :]).astype(BF16)
    acc = None
    for j in range(D_FF // D):
        u = jnp.maximum(jnp.dot(h, wu_ref[:, j * D:(j + 1) * D], preferred_element_type=F32), 0.0)
        term = jnp.dot((u * u).astype(BF16), wd_ref[j * D:(j + 1) * D, :], preferred_element_type=F32)
        acc = term if acc is None else acc + term
    o_ref[...] = x + mod_ref[5:6, :] * _rms(acc, gpost_ref[...])


def _ffn(x, mod_l, gpre, gpost, wu, wd, tm):
    B, S, D = x.shape
    tile = pl.BlockSpec((None, tm, D), lambda b, i: (b, i, 0))
    return pl.pallas_call(
        _ffn_kernel,
        grid=(B, S // tm),
        in_specs=[tile, pl.BlockSpec((None, 6, D), lambda b, i: (b, 0, 0)), _full(gpre.shape),
                  _full(gpost.shape), _full(wu.shape), _full(wd.shape)],
        out_specs=tile,
        out_shape=jax.ShapeDtypeStruct((B, S, D), F32),
        compiler_params=_params(("parallel", "parallel")),
        name="ffn",
    )(x, mod_l, gpre, gpost, wu, wd)


def _pad_heads(w, dh, to):
    K = w.shape[0]
    return jnp.pad(w.reshape(K, N_HEADS, dh), ((0, 0), (0, 0), (0, to - dh))).reshape(K, N_HEADS * to)


def _fox_selectors():
    part = jnp.arange(3)[:, None]
    hd = jnp.arange(N_HEADS)[None, :]
    rows = (part * 128 + hd).reshape(-1)
    selq = jnp.zeros((384, 512), F32).at[rows, (hd * 128 + 64 + part).reshape(-1)].set(1.0)
    selk = jnp.zeros((384, 512), F32).at[rows, (hd * 128 + 67 + part).reshape(-1)].set(-1.0)
    lane = jnp.arange(512) % 128
    cq = ((lane >= 67) & (lane < 70)).astype(F32).reshape(1, 512)
    ck = ((lane >= 64) & (lane < 67)).astype(F32).reshape(1, 512)
    return selq.astype(BF16), selk.astype(BF16), cq, ck


def _row(v):
    return v.reshape(1, -1)


def kernel(x, c, positions, w_in, mu_shift, w0, w_decay_up, a0, w_aaa_up, w_gate_up, k_k, k_a, r_k,
           ln_x_g, ln_x_b, b_forget, q_norm_g, w_q_up, kv_norm_g, w_kv_up, w_branch_a, w_branch_b,
           w_branch_c, w_out, w_mod, b_mod, norm_mix_pre, norm_mix_post, norm_ffn_pre, norm_ffn_post,
           w_ffn_up, w_ffn_down):
    B, S, D = x.shape
    L = w_in.shape[0]
    tm = min(512, S)
    tq = min(256, S)

    c8 = jnp.pad(c, ((0, 8 - B), (0, 0)))
    mod = _modulation(c8, w_mod, b_mod)[:, :B].reshape(L, B, 6, D)
    cos_t, sin_t = _rope_tables(positions, tm)
    selq, selk, cq, ck = _fox_selectors()

    for l in range(L):
        wi = w_in[l]
        w1 = jnp.concatenate([
            wi[:, :_FOX0],
            _pad_heads(wi[:, _FOX0:_FOX0 + 256], 64, 128) * (HEAD64 ** -0.5),
            _pad_heads(wi[:, _FOX0 + 256:_FOX0 + 512], 64, 128),
            _pad_heads(wi[:, _FOX0 + 512:_FOX0 + 768], 64, 128),
            jnp.pad(wi[:, _FOX0 + 768:_MLA0], ((0, 0), (0, 128 - N_HEADS))),
            wi[:, _MLA0:_MLA0 + Q_LORA_RANK + KV_LORA_RANK],
            jnp.pad(wi[:, _GATE0 - QK_ROPE_DIM:_GATE0], ((0, 0), (0, 128 - QK_ROPE_DIM))),
        ], axis=1).astype(BF16)
        bf = jnp.pad(_row(b_forget[l]), ((0, 0), (0, 128 - N_HEADS)))
        wq = (_pad_heads(w_q_up[l], QK_NOPE_DIM + QK_ROPE_DIM, 256)
              * ((QK_NOPE_DIM + QK_ROPE_DIM) ** -0.5)).astype(BF16)
        wkv = w_kv_up[l].astype(BF16)

        za, fq, fk, fv, mq, mk, mv = _inproj(
            x, mod[l], _row(norm_mix_pre[l]), w1, bf, selq, selk, cq, ck, _row(q_norm_g[l]), wq,
            _row(kv_norm_g[l]), wkv, cos_t, sin_t, tm)

        zeros64 = jnp.zeros((64, RWKV_WIDTH), F32)
        ya = _rwkv(za, _row(mu_shift[l]), _row(w0[l]),
                   jnp.concatenate([w_decay_up[l], zeros64], axis=0), _row(a0[l]),
                   jnp.concatenate([zeros64, w_aaa_up[l]], axis=0), w_gate_up[l],
                   _row(k_k[l]), _row(k_a[l]), _row(r_k[l]), _row(ln_x_g[l]), _row(ln_x_b[l]), tm)
        yb = _attention(fq, fk, fv, tq, "fox_attention")
        yc = _attention(mq, mk, mv, tq, "mla_attention")

        wb = jnp.pad(w_branch_b[l].reshape(N_HEADS, 64, D), ((0, 0), (0, 64), (0, 0))).reshape(512, D)
        x = _merge(x, mod[l], _row(norm_mix_pre[l]), _row(norm_mix_post[l]),
                   wi[:, _GATE0:].astype(BF16), ya, yb, yc, w_branch_a[l].astype(BF16),
                   wb.astype(BF16), w_branch_c[l].astype(BF16), w_out[l].astype(BF16), tm)
        x = _ffn(x, mod[l], _row(norm_ffn_pre[l]), _row(norm_ffn_post[l]),
                 w_ffn_up[l].astype(BF16), w_ffn_down[l].astype(BF16), tm)
    return x
```

```python
import functools

import jax
import jax.numpy as jnp
from jax import lax
from jax.experimental import pallas as pl
from jax.experimental.pallas import tpu as pltpu

F32 = jnp.float32
BF16 = jnp.bfloat16
HIGHEST = lax.Precision.HIGHEST

D_MODEL = 1024
N_HEADS = 4
HEAD64 = 64
RWKV_WIDTH = N_HEADS * HEAD64
QK_NOPE_DIM = 128
QK_ROPE_DIM = 64
V_HEAD_DIM = 128
Q_LORA_RANK = 384
KV_LORA_RANK = 256
ROPE_THETA = 10000.0
D_FF = 4 * D_MODEL
NORM_EPS = 1e-6
RWKV_LN_EPS = 64e-5
MASK_VALUE = -1e30
LOG2E = 1.4426950408889634
CHUNK = 64

_FOX0 = 1024
_MLA0 = _FOX0 + 3 * 256 + N_HEADS
_GATE0 = _MLA0 + Q_LORA_RANK + KV_LORA_RANK + QK_ROPE_DIM

_C_RWKV = 0
_C_FQ = 1024
_C_FK = 1536
_C_FV = 2048
_C_FF = 2304
_C_QL = 2432
_C_KVL = 2816
_C_KPE = 3072
_C_END = 3200

VMEM_LIMIT = 56 * 1024 * 1024


def _bdot(a, b):
    return jnp.dot(a.astype(BF16), b.astype(BF16), preferred_element_type=F32)


def _bdot_nt(a, b):
    return lax.dot_general(a.astype(BF16), b.astype(BF16), (((1,), (1,)), ((), ())),
                           preferred_element_type=F32)


def _bdot_tn(a, b):
    return lax.dot_general(a.astype(BF16), b.astype(BF16), (((0,), (0,)), ((), ())),
                           preferred_element_type=F32)


def _fdot(a, b):
    return jnp.dot(a, b, preferred_element_type=F32, precision=HIGHEST)


def _split2_dot(a, b_bf16):
    hi = a.astype(BF16)
    lo = (a - hi.astype(F32)).astype(BF16)
    return (jnp.dot(hi, b_bf16, preferred_element_type=F32)
            + jnp.dot(lo, b_bf16, preferred_element_type=F32))


def _split3(a):
    hi = a.astype(BF16)
    r1 = a - hi.astype(F32)
    mid = r1.astype(BF16)
    lo = (r1 - mid.astype(F32)).astype(BF16)
    return hi, mid, lo


def _rms(x, gain):
    return x * lax.rsqrt(jnp.mean(x * x, axis=-1, keepdims=True) + NORM_EPS) * gain


def _softplus(y):
    return jnp.maximum(y, 0.0) + jnp.log(1.0 + jnp.exp(-jnp.abs(y)))


def _sigmoid(y):
    return 1.0 / (1.0 + jnp.exp(-y))


def _iota(shape, dim):
    return lax.broadcasted_iota(jnp.int32, shape, dim)


def _full(shape):
    return pl.BlockSpec(shape, lambda *_: (0,) * len(shape))


def _params(sem):
    return pltpu.CompilerParams(dimension_semantics=sem, vmem_limit_bytes=VMEM_LIMIT)


def _mod_kernel(c_ref, w_ref, b_ref, o_ref):
    c = c_ref[...]
    o_ref[...] = _fdot(c * _sigmoid(c), w_ref[...]) + b_ref[...]


def _modulation(c8, w_mod, b_mod):
    L, D, D6 = w_mod.shape
    return pl.pallas_call(
        _mod_kernel,
        grid=(L, D6 // D),
        in_specs=[pl.BlockSpec((8, D), lambda l, j: (0, 0)),
                  pl.BlockSpec((None, D, D), lambda l, j: (l, 0, j)),
                  pl.BlockSpec((None, 1, D), lambda l, j: (l, 0, j))],
        out_specs=pl.BlockSpec((None, 8, D), lambda l, j: (l, 0, j)),
        out_shape=jax.ShapeDtypeStruct((L, 8, D6), F32),
        compiler_params=_params(("parallel", "parallel")),
        name="adaln_mod",
    )(c8, w_mod, b_mod.reshape(L, 1, D6))


def _rope_kernel(pos_ref, inv_ref, cos_ref, sin_ref):
    ang = pos_ref[...] * inv_ref[...]
    lane = _iota(ang.shape, 1)
    cos_ref[...] = jnp.where(lane < 64, jnp.cos(ang), 0.0)
    s = jnp.sin(ang)
    sin_ref[...] = jnp.where(lane < 32, -s, jnp.where(lane < 64, s, 0.0))


def _rope_tables(positions, tm):
    B, S = positions.shape
    inv_freq = ROPE_THETA ** (-jnp.arange(0, QK_ROPE_DIM, 2, dtype=F32) / QK_ROPE_DIM)
    inv128 = jnp.tile(inv_freq, 4).reshape(1, 128)
    pos = positions.astype(F32).reshape(B, S, 1)
    out = jax.ShapeDtypeStruct((B, S, 128), F32)
    return pl.pallas_call(
        _rope_kernel,
        grid=(B, S // tm),
        in_specs=[pl.BlockSpec((None, tm, 1), lambda b, i: (b, i, 0)), _full((1, 128))],
        out_specs=[pl.BlockSpec((None, tm, 128), lambda b, i: (b, i, 0))] * 2,
        out_shape=[out, out],
        compiler_params=_params(("parallel", "parallel")),
        name="rope_tables",
    )(pos, inv128)


def _rope128(x, cos_t, sin_t):
    lane = _iota(x.shape, 1)
    partner = jnp.where(lane < 32, pltpu.roll(x, 96, 1), pltpu.roll(x, 32, 1))
    return x * cos_t + partner * sin_t


def _inproj_kernel(x_ref, mod_ref, gpre_ref, w1_ref, bf_ref, selq_ref, selk_ref, cq_ref, ck_ref,
                   qg_ref, wq_ref, kvg_ref, wkv_ref, cos_ref, sin_ref,
                   za_ref, fq_ref, fk_ref, fv_ref, mq_ref, mk_ref, mv_ref, carry_ref):
    i = pl.program_id(1)
    tm = x_ref.shape[0]

    @pl.when(i == 0)
    def _():
        carry_ref[...] = jnp.zeros_like(carry_ref)

    x = x_ref[...]
    h = (_rms(x, gpre_ref[...]) * (1.0 + mod_ref[1:2, :]) + mod_ref[0:1, :]).astype(BF16)

    za_ref[...] = jnp.dot(h, w1_ref[:, _C_RWKV:_C_FQ], preferred_element_type=F32)

    f = jnp.dot(h, w1_ref[:, _C_FF:_C_QL], preferred_element_type=F32) + bf_ref[...]
    logf = -_softplus(-f)
    tri = (_iota((tm, tm), 0) >= _iota((tm, tm), 1)).astype(BF16)
    l_hi, l_mid, l_lo = _split3(logf)
    cum = (jnp.dot(tri, l_hi, preferred_element_type=F32)
           + jnp.dot(tri, l_mid, preferred_element_type=F32)
           + jnp.dot(tri, l_lo, preferred_element_type=F32)) + carry_ref[0:1, :]
    carry_ref[0:1, :] = cum[tm - 1:tm, :]
    cc = jnp.concatenate(_split3(cum * LOG2E), axis=1)
    fq = (jnp.dot(h, w1_ref[:, _C_FQ:_C_FK], preferred_element_type=F32)
          + jnp.dot(cc, selq_ref[...], preferred_element_type=F32) + cq_ref[...])
    fk = (jnp.dot(h, w1_ref[:, _C_FK:_C_FV], preferred_element_type=F32)
          + jnp.dot(cc, selk_ref[...], preferred_element_type=F32) + ck_ref[...])
    fv_t = jnp.dot(h, w1_ref[:, _C_FV:_C_FF], preferred_element_type=F32).T
    ones_rows = (_iota((16, tm), 0) == 0).astype(BF16)
    for hd in range(N_HEADS):
        sl = slice(128 * hd, 128 * hd + 128)
        fq_ref[hd] = fq[:, sl].astype(BF16)
        fk_ref[hd] = fk[:, sl].astype(BF16)
        fv_ref[hd] = jnp.concatenate([fv_t[64 * hd:64 * hd + 64].astype(BF16), ones_rows], axis=0)

    cos_t = cos_ref[...]
    sin_t = sin_ref[...]
    q_lat = jnp.dot(h, w1_ref[:, _C_QL:_C_KVL], preferred_element_type=F32)
    q = _bdot(_rms(q_lat, qg_ref[...]), wq_ref[...])
    kv_lat = jnp.dot(h, w1_ref[:, _C_KVL:_C_KPE], preferred_element_type=F32)
    kv = _bdot(_rms(kv_lat, kvg_ref[...]), wkv_ref[...])
    k_pe = _rope128(jnp.dot(h, w1_ref[:, _C_KPE:_C_END], preferred_element_type=F32), cos_t, sin_t)
    k_pe = k_pe.astype(BF16)
    for hd in range(N_HEADS):
        o = 256 * hd
        q_pe = _rope128(q[:, o + 128:o + 256], cos_t, sin_t)
        mq_ref[hd] = jnp.concatenate([q[:, o:o + 128], q_pe], axis=1).astype(BF16)
        mk_ref[hd] = jnp.concatenate([kv[:, o:o + 128].astype(BF16), k_pe], axis=1)
        mv_ref[hd] = jnp.concatenate([kv[:, o + 128:o + 256].T.astype(BF16), ones_rows], axis=0)


def _inproj(x, mod_l, gpre, w1, bf, selq, selk, cq, ck, qg, wq, kvg, wkv, cos_t, sin_t, tm):
    B, S, D = x.shape
    H = N_HEADS
    tile = lambda w: pl.BlockSpec((None, tm, w), lambda b, i: (b, i, 0))
    head = lambda w: pl.BlockSpec((None, H, tm, w), lambda b, i: (b, 0, i, 0))
    hs = lambda w: jax.ShapeDtypeStruct((B, H, S, w), BF16)
    head_t = lambda r: pl.BlockSpec((None, H, None, r, tm), lambda b, i: (b, 0, i, 0, 0))
    hs_t = lambda r: jax.ShapeDtypeStruct((B, H, S // tm, r, tm), BF16)
    return pl.pallas_call(
        _inproj_kernel,
        grid=(B, S // tm),
        in_specs=[tile(D), pl.BlockSpec((None, 6, D), lambda b, i: (b, 0, 0)), _full(gpre.shape),
                  _full(w1.shape), _full(bf.shape), _full(selq.shape), _full(selk.shape),
                  _full(cq.shape), _full(ck.shape), _full(qg.shape), _full(wq.shape),
                  _full(kvg.shape), _full(wkv.shape), tile(128), tile(128)],
        out_specs=[tile(1024), head(128), head(128), head_t(80), head(256), head(256), head_t(144)],
        out_shape=[jax.ShapeDtypeStruct((B, S, 1024), F32), hs(128), hs(128), hs_t(80),
                   hs(256), hs(256), hs_t(144)],
        scratch_shapes=[pltpu.VMEM((8, 128), F32)],
        compiler_params=_params(("parallel", "arbitrary")),
        name="in_proj",
    )(x, mod_l, gpre, w1, bf, selq, selk, cq, ck, qg, wq, kvg, wkv, cos_t, sin_t)


def _rwkv_kernel(z_ref, mu_ref, w0_ref, wd_ref, a0_ref, wa_ref, wg_ref, kk_ref, ka_ref, rk_ref,
                 lng_ref, lnb_ref, y_ref,
                 prev_ref, h_ref, rt_ref, at_ref, bt_ref, kt_ref, bh_ref, kh_ref, v_ref, gam_ref,
                 ys_ref):
    i = pl.program_id(1)
    tm = z_ref.shape[0]
    W = RWKV_WIDTH
    C = CHUNK

    @pl.when(i == 0)
    def _():
        prev_ref[...] = jnp.zeros_like(prev_ref)
        h_ref[...] = jnp.zeros_like(h_ref)

    z = z_ref[...]
    zprev = jnp.where(_iota((tm, 1), 0) == 0, prev_ref[0:1, :], pltpu.roll(z, 1, 0))
    prev_ref[0:1, :] = z[tm - 1:tm, :]
    zm = z + (zprev - z) * mu_ref[...]
    r = zm[:, 0:W]
    k = zm[:, W:2 * W]
    v = zm[:, 2 * W:3 * W]
    lora_in = zm[:, 3 * W:3 * W + 128]
    gd = zm[:, 3 * W + 128:3 * W + 256]
    w_log = -_softplus(-(w0_ref[...] + _fdot(jnp.tanh(lora_in), wd_ref[...]))) - 0.5
    lw = -jnp.exp(w_log)
    a = _sigmoid(a0_ref[...] + _fdot(lora_in, wa_ref[...]))
    g = _fdot(_sigmoid(gd), wg_ref[...])
    seg = ((_iota((W, W), 0) // HEAD64) == (_iota((W, W), 1) // HEAD64)).astype(BF16)
    kk = k * kk_ref[...]
    kk = kk / jnp.maximum(jnp.sqrt(_split2_dot(kk * kk, seg)), 1e-12)
    k2 = k * (1.0 + (a - 1.0) * ka_ref[...])

    tri = (_iota((C, C), 0) >= _iota((C, C), 1)).astype(F32)
    cls, ces = [], []
    for c in range(tm // C):
        cl_c = _fdot(tri, lw[c * C:(c + 1) * C, :])
        cls.append(cl_c)
        ces.append(jnp.broadcast_to(cl_c[C - 1:C, :], (C, W)))
    cl = jnp.concatenate(cls, axis=0)
    ce = jnp.concatenate(ces, axis=0)
    e_neg = jnp.exp(-cl)
    e_end = jnp.exp(ce - cl)
    rt_ref[...] = r * jnp.exp(cl)
    at_ref[...] = -kk * jnp.exp(cl - lw)
    bt_ref[...] = kk * a * e_neg
    kt_ref[...] = k2 * e_neg
    bh_ref[...] = kk * a * e_end
    kh_ref[...] = k2 * e_end
    v_ref[...] = v
    gam_ref[...] = jnp.exp(ce)

    lane_head = _iota((1, W), 1) // HEAD64
    row_t = _iota((W, W), 0) % C
    col_t = _iota((W, W), 1) % C
    strict_lower = row_t > col_t
    incl_lower = row_t >= col_t
    eye_mask = _iota((W, W), 0) == _iota((W, W), 1)
    eye = eye_mask.astype(F32)

    def stack(ref, rows):
        xc = ref[rows, :]
        return jnp.concatenate([jnp.where(lane_head == hd, xc, 0.0) for hd in range(N_HEADS)],
                               axis=0).astype(BF16)

    def chunk_step(c, hst):
        rows = pl.ds(pl.multiple_of(c * C, C), C)
        a_s, r_s, b_s, k_s = stack(at_ref, rows), stack(rt_ref, rows), stack(bt_ref, rows), stack(kt_ref, rows)
        v_s, bh_s, kh_s = stack(v_ref, rows), stack(bh_ref, rows), stack(kh_ref, rows)
        m = _bdot_nt(jnp.concatenate([a_s, r_s], axis=0), jnp.concatenate([b_s, k_s], axis=0))
        a_ab = jnp.where(strict_lower, m[:W, :W], 0.0)
        a_ak = jnp.where(strict_lower, m[:W, W:], 0.0)
        m_rb = jnp.where(incl_lower, m[W:, :W], 0.0)
        m_rk = jnp.where(incl_lower, m[W:, W:], 0.0)
        av = _bdot(jnp.concatenate([a_ak, m_rk], axis=0), v_s)
        t_inv = eye + a_ab
        p = a_ab
        for _ in range(5):
            p = _bdot(p, p)
            t_inv = t_inv + _bdot(t_inv, p)
        x = _bdot(t_inv, jnp.concatenate([a_s.astype(F32), av[:W]], axis=1))
        ry = _bdot(m_rb, x)
        r_p = r_s.astype(F32) + ry[:, :W]
        y_p = av[W:] + ry[:, W:]
        rhs = jnp.concatenate(
            [x.astype(BF16), jnp.concatenate([jnp.zeros((W, W), BF16), v_s], axis=1)], axis=0)
        pg = _bdot_tn(jnp.concatenate([bh_s, kh_s], axis=0), rhs)
        hb = hst.astype(BF16)
        ys = _bdot(r_p, hb) + y_p
        ys_ref[rows, :] = ys[0:C] + ys[C:2 * C] + ys[2 * C:3 * C] + ys[3 * C:4 * C]
        gam_row = gam_ref[pl.ds(pl.multiple_of(c * C, C), 1), :]
        gam_col = jnp.sum(jnp.where(eye_mask, gam_row, 0.0), axis=1, keepdims=True)
        return gam_col * hst + _bdot(pg[:, :W], hb) + pg[:, W:]

    h_ref[...] = lax.fori_loop(0, tm // C, chunk_step, h_ref[...])

    y = ys_ref[...]
    inv = 1.0 / HEAD64
    mean = _split2_dot(y, seg) * inv
    d = y - mean
    var = _split2_dot(d * d, seg) * inv
    yn = d * lax.rsqrt(var + RWKV_LN_EPS) * lng_ref[...] + lnb_ref[...]
    bonus = _split2_dot(r * k2 * rk_ref[...], seg) * v
    y_ref[...] = ((yn + bonus) * g).astype(BF16)


def _rwkv(za, mu, w0, wd, a0, wa, wg, k_k, k_a, r_k, ln_g, ln_b, tm):
    B, S, _ = za.shape
    W = RWKV_WIDTH
    smalls = [mu, w0, wd, a0, wa, wg, k_k, k_a, r_k, ln_g, ln_b]
    return pl.pallas_call(
        _rwkv_kernel,
        grid=(B, S // tm),
        in_specs=[pl.BlockSpec((None, tm, 1024), lambda b, i: (b, i, 0))] + [_full(s.shape) for s in smalls],
        out_specs=pl.BlockSpec((None, tm, W), lambda b, i: (b, i, 0)),
        out_shape=jax.ShapeDtypeStruct((B, S, W), BF16),
        scratch_shapes=[pltpu.VMEM((8, 1024), F32), pltpu.VMEM((W, W), F32)]
                       + [pltpu.VMEM((tm, W), F32)] * 9,
        compiler_params=_params(("parallel", "arbitrary")),
        name="rwkv7_chunked",
    )(za, *smalls)


def _attn_kernel(q_ref, k_ref, vt_ref, o_ref, acc_ref, m_ref, s_ref, *, dv):
    qi = pl.program_id(1)
    H, tq, _ = q_ref.shape
    key_le_query = _iota((tq, tq), 0) <= _iota((tq, tq), 1)
    acc_ref[...] = jnp.zeros_like(acc_ref)
    m_ref[...] = jnp.full_like(m_ref, MASK_VALUE)

    def step(j, masked):
        rows = pl.ds(pl.multiple_of(j * tq, tq), tq)
        m_news, alphas = [], []
        for hd in range(H):
            st = _bdot_nt(k_ref[hd, rows, :], q_ref[hd])
            if masked:
                st = jnp.where(key_le_query, st, MASK_VALUE)
            s_ref[hd] = st
            m_old = m_ref[hd, 0:1, :]
            m_new = jnp.maximum(m_old, jnp.max(st, axis=0, keepdims=True))
            m_ref[hd, 0:1, :] = m_new
            m_news.append(m_new)
            alphas.append(jnp.exp2(m_old - m_new))
        for hd in range(H):
            p = jnp.exp2(s_ref[hd] - m_news[hd]).astype(BF16)
            acc_ref[hd] = alphas[hd] * acc_ref[hd] + jnp.dot(vt_ref[hd, j], p, preferred_element_type=F32)

    def body(j, carry):
        step(j, False)
        return carry

    lax.fori_loop(0, qi, body, 0)
    step(qi, True)
    outs = []
    for hd in range(H):
        acc = acc_ref[hd]
        outs.append(acc[:dv] * (1.0 / acc[dv:dv + 1]))
    o_ref[...] = jnp.concatenate(outs, axis=0).T.astype(BF16)


def _attention(q, k, vt, dv, name):
    B, H, S, dq = q.shape
    _, _, nq, dva, tq = vt.shape
    return pl.pallas_call(
        functools.partial(_attn_kernel, dv=dv),
        grid=(B, nq),
        in_specs=[pl.BlockSpec((None, H, tq, dq), lambda b, i: (b, 0, i, 0)),
                  pl.BlockSpec((None, H, S, dq), lambda b, i: (b, 0, 0, 0)),
                  pl.BlockSpec((None, H, nq, dva, tq), lambda b, i: (b, 0, 0, 0, 0))],
        out_specs=pl.BlockSpec((None, tq, H * dv), lambda b, i: (b, i, 0)),
        out_shape=jax.ShapeDtypeStruct((B, S, H * dv), BF16),
        scratch_shapes=[pltpu.VMEM((H, dva, tq), F32), pltpu.VMEM((H, 8, tq), F32),
                        pltpu.VMEM((H, tq, tq), F32)],
        compiler_params=_params(("parallel", "arbitrary")),
        name=name,
    )(q, k, vt)


def _merge_kernel(x_ref, mod_ref, gpre_ref, gpost_ref, wg_ref, ya_ref, yb_ref, yc_ref,
                  wa_ref, wb_ref, wc_ref, wo_ref, o_ref):
    D = D_MODEL
    x = x_ref[...]
    h = (_rms(x, gpre_ref[...]) * (1.0 + mod_ref[1:2, :]) + mod_ref[0:1, :]).astype(BF16)
    merged = None
    for j, (y_ref, w_ref) in enumerate(((ya_ref, wa_ref), (yb_ref, wb_ref), (yc_ref, wc_ref))):
        gate = _sigmoid(jnp.dot(h, wg_ref[:, j * D:(j + 1) * D], preferred_element_type=F32))
        term = gate * jnp.dot(y_ref[...], w_ref[...], preferred_element_type=F32)
        merged = term if merged is None else merged + term
    out = _bdot(merged, wo_ref[...])
    o_ref[...] = x + mod_ref[2:3, :] * _rms(out, gpost_ref[...])


def _merge(x, mod_l, gpre, gpost, wg, ya, yb, yc, wa, wb, wc, wo, tm):
    B, S, D = x.shape
    tile = lambda w: pl.BlockSpec((None, tm, w), lambda b, i: (b, i, 0))
    return pl.pallas_call(
        _merge_kernel,
        grid=(B, S // tm),
        in_specs=[tile(D), pl.BlockSpec((None, 6, D), lambda b, i: (b, 0, 0)), _full(gpre.shape),
                  _full(gpost.shape), _full(wg.shape), tile(ya.shape[-1]), tile(yb.shape[-1]),
                  tile(yc.shape[-1]), _full(wa.shape), _full(wb.shape), _full(wc.shape),
                  _full(wo.shape)],
        out_specs=tile(D),
        out_shape=jax.ShapeDtypeStruct((B, S, D), F32),
        compiler_params=_params(("parallel", "parallel")),
        name="merge_out",
    )(x, mod_l, gpre, gpost, wg, ya, yb, yc, wa, wb, wc, wo)


def _ffn_kernel(x_ref, mod_ref, gpre_ref, gpost_ref, wu_ref, wd_ref, o_ref):
    D = D_MODEL
    x = x_ref[...]
    h = (_rms(x, gpre_ref[...]) * (1.0 + mod_ref[4:5, :]) + mod_ref[3:4, :]).astype(BF16)
    acc = None
    for j in range(D_FF // D):
        u = jnp.maximum(jnp.dot(h, wu_ref[:, j * D:(j + 1) * D], preferred_element_type=F32), 0.0)
        term = jnp.dot((u * u).astype(BF16), wd_ref[j * D:(j + 1) * D, :], preferred_element_type=F32)
        acc = term if acc is None else acc + term
    o_ref[...] = x + mod_ref[5:6, :] * _rms(acc, gpost_ref[...])


def _ffn(x, mod_l, gpre, gpost, wu, wd, tm):
    B, S, D = x.shape
    tile = pl.BlockSpec((None, tm, D), lambda b, i: (b, i, 0))
    return pl.pallas_call(
        _ffn_kernel,
        grid=(B, S // tm),
        in_specs=[tile, pl.BlockSpec((None, 6, D), lambda b, i: (b, 0, 0)), _full(gpre.shape),
                  _full(gpost.shape), _full(wu.shape), _full(wd.shape)],
        out_specs=tile,
        out_shape=jax.ShapeDtypeStruct((B, S, D), F32),
        compiler_params=_params(("parallel", "parallel")),
        name="ffn",
    )(x, mod_l, gpre, gpost, wu, wd)


def _pad_heads(w, dh, to):
    K = w.shape[0]
    return jnp.pad(w.reshape(K, N_HEADS, dh), ((0, 0), (0, 0), (0, to - dh))).reshape(K, N_HEADS * to)


def _fox_selectors():
    part = jnp.arange(3)[:, None]
    hd = jnp.arange(N_HEADS)[None, :]
    rows = (part * 128 + hd).reshape(-1)
    selq = jnp.zeros((384, 512), F32).at[rows, (hd * 128 + 64 + part).reshape(-1)].set(1.0)
    selk = jnp.zeros((384, 512), F32).at[rows, (hd * 128 + 67 + part).reshape(-1)].set(-1.0)
    lane = jnp.arange(512) % 128
    cq = ((lane >= 67) & (lane < 70)).astype(F32).reshape(1, 512)
    ck = ((lane >= 64) & (lane < 67)).astype(F32).reshape(1, 512)
    return selq.astype(BF16), selk.astype(BF16), cq, ck


def _row(v):
    return v.reshape(1, -1)


def kernel(x, c, positions, w_in, mu_shift, w0, w_decay_up, a0, w_aaa_up, w_gate_up, k_k, k_a, r_k,
           ln_x_g, ln_x_b, b_forget, q_norm_g, w_q_up, kv_norm_g, w_kv_up, w_branch_a, w_branch_b,
           w_branch_c, w_out, w_mod, b_mod, norm_mix_pre, norm_mix_post, norm_ffn_pre, norm_ffn_post,
           w_ffn_up, w_ffn_down):
    B, S, D = x.shape
    L = w_in.shape[0]
    tm = min(512, S)

    c8 = jnp.pad(c, ((0, 8 - B), (0, 0)))
    mod = _modulation(c8, w_mod, b_mod)[:, :B].reshape(L, B, 6, D)
    cos_t, sin_t = _rope_tables(positions, tm)
    selq, selk, cq, ck = _fox_selectors()

    for l in range(L):
        wi = w_in[l]
        w1 = jnp.concatenate([
            wi[:, :_FOX0],
            _pad_heads(wi[:, _FOX0:_FOX0 + 256], 64, 128) * (HEAD64 ** -0.5 * LOG2E),
            _pad_heads(wi[:, _FOX0 + 256:_FOX0 + 512], 64, 128),
            wi[:, _FOX0 + 512:_FOX0 + 768],
            jnp.pad(wi[:, _FOX0 + 768:_MLA0], ((0, 0), (0, 128 - N_HEADS))),
            wi[:, _MLA0:_MLA0 + Q_LORA_RANK + KV_LORA_RANK],
            jnp.pad(wi[:, _GATE0 - QK_ROPE_DIM:_GATE0], ((0, 0), (0, 128 - QK_ROPE_DIM))),
        ], axis=1).astype(BF16)
        bf = jnp.pad(_row(b_forget[l]), ((0, 0), (0, 128 - N_HEADS)))
        wq = (_pad_heads(w_q_up[l], QK_NOPE_DIM + QK_ROPE_DIM, 256)
              * ((QK_NOPE_DIM + QK_ROPE_DIM) ** -0.5 * LOG2E)).astype(BF16)
        wkv = w_kv_up[l].astype(BF16)

        za, fq, fk, fv, mq, mk, mv = _inproj(
            x, mod[l], _row(norm_mix_pre[l]), w1, bf, selq, selk, cq, ck, _row(q_norm_g[l]), wq,
            _row(kv_norm_g[l]), wkv, cos_t, sin_t, tm)

        zeros64 = jnp.zeros((64, RWKV_WIDTH), F32)
        ya = _rwkv(za, _row(mu_shift[l]), _row(w0[l]),
                   jnp.concatenate([w_decay_up[l], zeros64], axis=0), _row(a0[l]),
                   jnp.concatenate([zeros64, w_aaa_up[l]], axis=0), w_gate_up[l],
                   _row(k_k[l]), _row(k_a[l]), _row(r_k[l]), _row(ln_x_g[l]), _row(ln_x_b[l]), tm)
        yb = _attention(fq, fk, fv, HEAD64, "fox_attention")
        yc = _attention(mq, mk, mv, V_HEAD_DIM, "mla_attention")

        x = _merge(x, mod[l], _row(norm_mix_pre[l]), _row(norm_mix_post[l]),
                   wi[:, _GATE0:].astype(BF16), ya, yb, yc, w_branch_a[l].astype(BF16),
                   w_branch_b[l].astype(BF16), w_branch_c[l].astype(BF16), w_out[l].astype(BF16), tm)
        x = _ffn(x, mod[l], _row(norm_ffn_pre[l]), _row(norm_ffn_post[l]),
                 w_ffn_up[l].astype(BF16), w_ffn_down[l].astype(BF16), tm)
    return x
```

```python
import functools

import jax
import jax.numpy as jnp
from jax import lax
from jax.experimental import pallas as pl
from jax.experimental.pallas import tpu as pltpu

F32 = jnp.float32
BF16 = jnp.bfloat16
HIGHEST = lax.Precision.HIGHEST

D_MODEL = 1024
N_HEADS = 4
HEAD64 = 64
RWKV_WIDTH = N_HEADS * HEAD64
QK_NOPE_DIM = 128
QK_ROPE_DIM = 64
V_HEAD_DIM = 128
Q_LORA_RANK = 384
KV_LORA_RANK = 256
ROPE_THETA = 10000.0
D_FF = 4 * D_MODEL
NORM_EPS = 1e-6
RWKV_LN_EPS = 64e-5
MASK_VALUE = -1e30
LOG2E = 1.4426950408889634
CHUNK = 64

_FOX0 = 1024
_MLA0 = _FOX0 + 3 * 256 + N_HEADS
_GATE0 = _MLA0 + Q_LORA_RANK + KV_LORA_RANK + QK_ROPE_DIM

_C_RWKV = 0
_C_FQ = 1024
_C_FK = 1536
_C_FV = 2048
_C_FF = 2304
_C_QL = 2432
_C_KVL = 2816
_C_KPE = 3072
_C_END = 3200

VMEM_LIMIT = 56 * 1024 * 1024


def _bdot(a, b):
    return jnp.dot(a.astype(BF16), b.astype(BF16), preferred_element_type=F32)


def _bdot_nt(a, b):
    return lax.dot_general(a.astype(BF16), b.astype(BF16), (((1,), (1,)), ((), ())),
                           preferred_element_type=F32)


def _bdot_tn(a, b):
    return lax.dot_general(a.astype(BF16), b.astype(BF16), (((0,), (0,)), ((), ())),
                           preferred_element_type=F32)


def _fdot(a, b):
    return jnp.dot(a, b, preferred_element_type=F32, precision=HIGHEST)


def _split2_dot(a, b_bf16):
    hi = a.astype(BF16)
    lo = (a - hi.astype(F32)).astype(BF16)
    return (jnp.dot(hi, b_bf16, preferred_element_type=F32)
            + jnp.dot(lo, b_bf16, preferred_element_type=F32))


def _split3(a):
    hi = a.astype(BF16)
    r1 = a - hi.astype(F32)
    mid = r1.astype(BF16)
    lo = (r1 - mid.astype(F32)).astype(BF16)
    return hi, mid, lo


def _rms(x, gain):
    return x * lax.rsqrt(jnp.mean(x * x, axis=-1, keepdims=True) + NORM_EPS) * gain


def _softplus(y):
    return jnp.maximum(y, 0.0) + jnp.log(1.0 + jnp.exp(-jnp.abs(y)))


def _sigmoid(y):
    return 1.0 / (1.0 + jnp.exp(-y))


def _iota(shape, dim):
    return lax.broadcasted_iota(jnp.int32, shape, dim)


def _full(shape):
    return pl.BlockSpec(shape, lambda *_: (0,) * len(shape))


def _params(sem):
    return pltpu.CompilerParams(dimension_semantics=sem, vmem_limit_bytes=VMEM_LIMIT)


def _mod_kernel(c_ref, w_ref, b_ref, o_ref):
    c = c_ref[...]
    o_ref[...] = _fdot(c * _sigmoid(c), w_ref[...]) + b_ref[...]


def _modulation(c8, w_mod, b_mod):
    L, D, D6 = w_mod.shape
    return pl.pallas_call(
        _mod_kernel,
        grid=(L, D6 // D),
        in_specs=[pl.BlockSpec((8, D), lambda l, j: (0, 0)),
                  pl.BlockSpec((None, D, D), lambda l, j: (l, 0, j)),
                  pl.BlockSpec((None, 1, D), lambda l, j: (l, 0, j))],
        out_specs=pl.BlockSpec((None, 8, D), lambda l, j: (l, 0, j)),
        out_shape=jax.ShapeDtypeStruct((L, 8, D6), F32),
        compiler_params=_params(("parallel", "parallel")),
        name="adaln_mod",
    )(c8, w_mod, b_mod.reshape(L, 1, D6))


def _rope_kernel(pos_ref, inv_ref, cos_ref, sin_ref):
    ang = pos_ref[...] * inv_ref[...]
    lane = _iota(ang.shape, 1)
    cos_ref[...] = jnp.where(lane < 64, jnp.cos(ang), 0.0)
    s = jnp.sin(ang)
    sin_ref[...] = jnp.where(lane < 32, -s, jnp.where(lane < 64, s, 0.0))


def _rope_tables(positions, tm):
    B, S = positions.shape
    inv_freq = ROPE_THETA ** (-jnp.arange(0, QK_ROPE_DIM, 2, dtype=F32) / QK_ROPE_DIM)
    inv128 = jnp.tile(inv_freq, 4).reshape(1, 128)
    pos = positions.astype(F32).reshape(B, S, 1)
    out = jax.ShapeDtypeStruct((B, S, 128), F32)
    return pl.pallas_call(
        _rope_kernel,
        grid=(B, S // tm),
        in_specs=[pl.BlockSpec((None, tm, 1), lambda b, i: (b, i, 0)), _full((1, 128))],
        out_specs=[pl.BlockSpec((None, tm, 128), lambda b, i: (b, i, 0))] * 2,
        out_shape=[out, out],
        compiler_params=_params(("parallel", "parallel")),
        name="rope_tables",
    )(pos, inv128)


def _rope128(x, cos_t, sin_t):
    lane = _iota(x.shape, 1)
    partner = jnp.where(lane < 32, pltpu.roll(x, 96, 1), pltpu.roll(x, 32, 1))
    return x * cos_t + partner * sin_t


def _inproj_kernel(x_ref, mod_ref, gpre_ref, w1_ref, bf_ref, selq_ref, selk_ref, cq_ref, ck_ref,
                   qg_ref, wq_ref, kvg_ref, wkv_ref, cos_ref, sin_ref,
                   za_ref, fq_ref, fk_ref, fv_ref, mq_ref, mk_ref, mv_ref, carry_ref):
    i = pl.program_id(1)
    tm = x_ref.shape[0]

    @pl.when(i == 0)
    def _():
        carry_ref[...] = jnp.zeros_like(carry_ref)

    x = x_ref[...]
    h = (_rms(x, gpre_ref[...]) * (1.0 + mod_ref[1:2, :]) + mod_ref[0:1, :]).astype(BF16)

    za_ref[...] = jnp.dot(h, w1_ref[:, _C_RWKV:_C_FQ], preferred_element_type=F32)

    f = jnp.dot(h, w1_ref[:, _C_FF:_C_QL], preferred_element_type=F32) + bf_ref[...]
    logf = -_softplus(-f)
    tri = (_iota((tm, tm), 0) >= _iota((tm, tm), 1)).astype(BF16)
    l_hi, l_mid, l_lo = _split3(logf)
    cum = (jnp.dot(tri, l_hi, preferred_element_type=F32)
           + jnp.dot(tri, l_mid, preferred_element_type=F32)
           + jnp.dot(tri, l_lo, preferred_element_type=F32)) + carry_ref[0:1, :]
    carry_ref[0:1, :] = cum[tm - 1:tm, :]
    cc = jnp.concatenate(_split3(cum * LOG2E), axis=1)
    fq = (jnp.dot(h, w1_ref[:, _C_FQ:_C_FK], preferred_element_type=F32)
          + jnp.dot(cc, selq_ref[...], preferred_element_type=F32) + cq_ref[...])
    fk = (jnp.dot(h, w1_ref[:, _C_FK:_C_FV], preferred_element_type=F32)
          + jnp.dot(cc, selk_ref[...], preferred_element_type=F32) + ck_ref[...])
    fv_t = jnp.dot(h, w1_ref[:, _C_FV:_C_FF], preferred_element_type=F32).T
    ones_rows = (_iota((16, tm), 0) == 0).astype(BF16)
    for hd in range(N_HEADS):
        sl = slice(128 * hd, 128 * hd + 128)
        fq_ref[hd] = fq[:, sl].astype(BF16)
        fk_ref[hd] = fk[:, sl].astype(BF16)
        fv_ref[hd] = jnp.concatenate([fv_t[64 * hd:64 * hd + 64].astype(BF16), ones_rows], axis=0)

    cos_t = cos_ref[...]
    sin_t = sin_ref[...]
    q_lat = jnp.dot(h, w1_ref[:, _C_QL:_C_KVL], preferred_element_type=F32)
    q = _bdot(_rms(q_lat, qg_ref[...]), wq_ref[...])
    kv_lat = jnp.dot(h, w1_ref[:, _C_KVL:_C_KPE], preferred_element_type=F32)
    kv = _bdot(_rms(kv_lat, kvg_ref[...]), wkv_ref[...])
    k_pe = _rope128(jnp.dot(h, w1_ref[:, _C_KPE:_C_END], preferred_element_type=F32), cos_t, sin_t)
    k_pe = k_pe.astype(BF16)
    for hd in range(N_HEADS):
        o = 256 * hd
        q_pe = _rope128(q[:, o + 128:o + 256], cos_t, sin_t)
        mq_ref[hd] = jnp.concatenate([q[:, o:o + 128], q_pe], axis=1).astype(BF16)
        mk_ref[hd] = jnp.concatenate([kv[:, o:o + 128].astype(BF16), k_pe], axis=1)
        mv_ref[hd] = jnp.concatenate([kv[:, o + 128:o + 256].T.astype(BF16), ones_rows], axis=0)


def _inproj(x, mod_l, gpre, w1, bf, selq, selk, cq, ck, qg, wq, kvg, wkv, cos_t, sin_t, tm):
    B, S, D = x.shape
    H = N_HEADS
    tile = lambda w: pl.BlockSpec((None, tm, w), lambda b, i: (b, i, 0))
    head = lambda w: pl.BlockSpec((None, H, tm, w), lambda b, i: (b, 0, i, 0))
    hs = lambda w: jax.ShapeDtypeStruct((B, H, S, w), BF16)
    head_t = lambda r: pl.BlockSpec((None, H, None, r, tm), lambda b, i: (b, 0, i, 0, 0))
    hs_t = lambda r: jax.ShapeDtypeStruct((B, H, S // tm, r, tm), BF16)
    return pl.pallas_call(
        _inproj_kernel,
        grid=(B, S // tm),
        in_specs=[tile(D), pl.BlockSpec((None, 6, D), lambda b, i: (b, 0, 0)), _full(gpre.shape),
                  _full(w1.shape), _full(bf.shape), _full(selq.shape), _full(selk.shape),
                  _full(cq.shape), _full(ck.shape), _full(qg.shape), _full(wq.shape),
                  _full(kvg.shape), _full(wkv.shape), tile(128), tile(128)],
        out_specs=[tile(1024), head(128), head(128), head_t(80), head(256), head(256), head_t(144)],
        out_shape=[jax.ShapeDtypeStruct((B, S, 1024), F32), hs(128), hs(128), hs_t(80),
                   hs(256), hs(256), hs_t(144)],
        scratch_shapes=[pltpu.VMEM((8, 128), F32)],
        compiler_params=_params(("parallel", "arbitrary")),
        name="in_proj",
    )(x, mod_l, gpre, w1, bf, selq, selk, cq, ck, qg, wq, kvg, wkv, cos_t, sin_t)


def _rwkv_kernel(z_ref, mu_ref, w0_ref, wd_ref, a0_ref, wa_ref, wg_ref, kk_ref, ka_ref, rk_ref,
                 lng_ref, lnb_ref, y_ref,
                 prev_ref, h_ref, rt_ref, at_ref, bt_ref, kt_ref, bh_ref, kh_ref, v_ref, gam_ref,
                 ys_ref, yp_ref, gw_ref, gw2_ref, rp_ref, pw_ref):
    i = pl.program_id(1)
    tm = z_ref.shape[0]
    W = RWKV_WIDTH
    C = CHUNK

    @pl.when(i == 0)
    def _():
        prev_ref[...] = jnp.zeros_like(prev_ref)
        h_ref[...] = jnp.zeros_like(h_ref)

    z = z_ref[...]
    zprev = jnp.where(_iota((tm, 1), 0) == 0, prev_ref[0:1, :], pltpu.roll(z, 1, 0))
    prev_ref[0:1, :] = z[tm - 1:tm, :]
    zm = z + (zprev - z) * mu_ref[...]
    r = zm[:, 0:W]
    k = zm[:, W:2 * W]
    v = zm[:, 2 * W:3 * W]
    lora_in = zm[:, 3 * W:3 * W + 128]
    gd = zm[:, 3 * W + 128:3 * W + 256]
    w_log = -_softplus(-(w0_ref[...] + _bdot(jnp.tanh(lora_in), wd_ref[...]))) - 0.5
    lw = -jnp.exp(w_log)
    a = _sigmoid(a0_ref[...] + _bdot(lora_in, wa_ref[...]))
    g = _bdot(_sigmoid(gd), wg_ref[...])
    seg = ((_iota((W, W), 0) // HEAD64) == (_iota((W, W), 1) // HEAD64)).astype(BF16)
    kk = k * kk_ref[...]
    kk = kk / jnp.maximum(jnp.sqrt(_split2_dot(kk * kk, seg)), 1e-12)
    k2 = k * (1.0 + (a - 1.0) * ka_ref[...])

    tri = (_iota((C, C), 0) >= _iota((C, C), 1)).astype(BF16)
    lw3 = _split3(lw)
    cls, ces = [], []
    for c in range(tm // C):
        cl_c = sum(jnp.dot(tri, part[c * C:(c + 1) * C, :], preferred_element_type=F32) for part in lw3)
        cls.append(cl_c)
        ces.append(jnp.broadcast_to(cl_c[C - 1:C, :], (C, W)))
    cl = jnp.concatenate(cls, axis=0)
    ce = jnp.concatenate(ces, axis=0)
    e_neg = jnp.exp(-cl)
    e_end = jnp.exp(ce - cl)
    rt_ref[...] = r * jnp.exp(cl)
    at_ref[...] = -kk * jnp.exp(cl - lw)
    bt_ref[...] = kk * a * e_neg
    kt_ref[...] = k2 * e_neg
    bh_ref[...] = kk * a * e_end
    kh_ref[...] = k2 * e_end
    v_ref[...] = v
    gam_ref[...] = jnp.exp(ce)

    block_mask = (_iota((W, W), 0) // C) == (_iota((W, W), 1) // C)
    t_row = _iota((C, W), 0)
    s_col = _iota((C, W), 1) % C
    strict_lower = t_row > s_col
    incl_lower = t_row >= s_col
    eye_w = (t_row == s_col).astype(F32)

    def block(x):
        return jnp.where(block_mask, jnp.concatenate([x] * N_HEADS, axis=0), 0.0).astype(BF16)

    def fold(x_bd):
        return x_bd[0:C] + x_bd[C:2 * C] + x_bd[2 * C:3 * C] + x_bd[3 * C:4 * C]

    group = 4

    def chunk_group(gi, carry):
        rows = [pl.ds(pl.multiple_of((gi * group + c) * C, C), C) for c in range(group)]
        cs = range(group)
        a_n = [at_ref[rows[c], :] for c in cs]
        a_s = [block(a_n[c]) for c in cs]
        m = [_bdot_nt(jnp.concatenate([a_n[c], rt_ref[rows[c], :]], axis=0),
                      jnp.concatenate([block(bt_ref[rows[c], :]), block(kt_ref[rows[c], :])], axis=0))
             for c in cs]
        a_ab = [jnp.where(strict_lower, m[c][:C, :W], 0.0) for c in cs]
        m_rb = [jnp.where(incl_lower, m[c][C:, :W], 0.0) for c in cs]
        av = [_bdot(jnp.concatenate([jnp.where(strict_lower, m[c][:C, W:], 0.0),
                                     jnp.where(incl_lower, m[c][C:, W:], 0.0)], axis=0),
                    block(v_ref[rows[c], :])) for c in cs]
        t_inv = [eye_w + a_ab[c] for c in cs]
        p = a_ab
        for _ in range(5):
            p = [_bdot(p[c], block(p[c])) for c in cs]
            t_inv = [t_inv[c] + _bdot(t_inv[c], block(p[c])) for c in cs]
        x = [_bdot(t_inv[c], jnp.concatenate([a_s[c], block(av[c][:C])], axis=1)) for c in cs]
        ry = [_bdot(m_rb[c], jnp.concatenate([block(x[c][:, :W]), block(x[c][:, W:])], axis=1)) for c in cs]
        for c in cs:
            rp_ref[rows[c], :] = (rt_ref[rows[c], :] + ry[c][:, :W]).astype(BF16)
            yp_ref[rows[c], :] = av[c][C:] + ry[c][:, W:]
            bh_n = bh_ref[rows[c], :]
            p_bd = jnp.where(block_mask, _bdot_tn(bh_n, x[c][:, :W]), 0.0)
            g_bd = jnp.where(block_mask,
                             _bdot_tn(jnp.concatenate([bh_n, kh_ref[rows[c], :]], axis=0),
                                      jnp.concatenate([x[c][:, W:], v_ref[rows[c], :]], axis=0)), 0.0)
            pw_ref[rows[c], :] = fold(p_bd).astype(BF16)
            gw_ref[rows[c], :] = fold(g_bd)
            gam_row = gam_ref[pl.ds(pl.multiple_of((gi * group + c) * C, C), 1), :]
            gw2_ref[rows[c], :] = _split2_dot(eye_w * gam_row, seg)
        return carry

    lax.fori_loop(0, tm // (C * group), chunk_group, 0)

    h_w = h_ref[...]
    for c in range(tm // C):
        rows = slice(c * C, (c + 1) * C)
        out = _bdot(jnp.concatenate([rp_ref[rows, :], pw_ref[rows, :]], axis=0), block(h_w))
        ys_ref[rows, :] = out[:C] + yp_ref[rows, :]
        h_w = gw2_ref[rows, :] * h_w + out[C:] + gw_ref[rows, :]
    h_ref[...] = h_w

    y = ys_ref[...]
    inv = 1.0 / HEAD64
    mean = _split2_dot(y, seg) * inv
    d = y - mean
    var = _split2_dot(d * d, seg) * inv
    yn = d * lax.rsqrt(var + RWKV_LN_EPS) * lng_ref[...] + lnb_ref[...]
    bonus = _split2_dot(r * k2 * rk_ref[...], seg) * v
    y_ref[...] = ((yn + bonus) * g).astype(BF16)


def _rwkv(za, mu, w0, wd, a0, wa, wg, k_k, k_a, r_k, ln_g, ln_b, tm):
    B, S, _ = za.shape
    W = RWKV_WIDTH
    smalls = [mu, w0, wd, a0, wa, wg, k_k, k_a, r_k, ln_g, ln_b]
    return pl.pallas_call(
        _rwkv_kernel,
        grid=(B, S // tm),
        in_specs=[pl.BlockSpec((None, tm, 1024), lambda b, i: (b, i, 0))] + [_full(s.shape) for s in smalls],
        out_specs=pl.BlockSpec((None, tm, W), lambda b, i: (b, i, 0)),
        out_shape=jax.ShapeDtypeStruct((B, S, W), BF16),
        scratch_shapes=[pltpu.VMEM((8, 1024), F32), pltpu.VMEM((CHUNK, W), F32)]
                       + [pltpu.VMEM((tm, W), F32)] * 12 + [pltpu.VMEM((tm, W), BF16)] * 2,
        compiler_params=_params(("parallel", "arbitrary")),
        name="rwkv7_chunked",
    )(za, *smalls)


def _attn_kernel(q_ref, k_ref, vt_ref, o_ref, acc_ref, m_ref, s_ref, *, dv):
    qi = pl.program_id(1)
    H, tq, _ = q_ref.shape
    key_le_query = _iota((tq, tq), 0) <= _iota((tq, tq), 1)
    acc_ref[...] = jnp.zeros_like(acc_ref)
    m_ref[...] = jnp.full_like(m_ref, MASK_VALUE)

    def step(j, masked):
        rows = pl.ds(pl.multiple_of(j * tq, tq), tq)
        m_news, alphas = [], []
        for hd in range(H):
            st = _bdot_nt(k_ref[hd, rows, :], q_ref[hd])
            if masked:
                st = jnp.where(key_le_query, st, MASK_VALUE)
            s_ref[hd] = st
            m_old = m_ref[hd, 0:1, :]
            m_new = jnp.maximum(m_old, jnp.max(st, axis=0, keepdims=True))
            m_ref[hd, 0:1, :] = m_new
            m_news.append(m_new)
            alphas.append(jnp.exp2(m_old - m_new))
        for hd in range(H):
            p = jnp.exp2(s_ref[hd] - m_news[hd]).astype(BF16)
            acc_ref[hd] = alphas[hd] * acc_ref[hd] + jnp.dot(vt_ref[hd, j], p, preferred_element_type=F32)

    def body(j, carry):
        step(j, False)
        return carry

    lax.fori_loop(0, qi, body, 0)
    step(qi, True)
    outs = []
    for hd in range(H):
        acc = acc_ref[hd]
        outs.append(acc[:dv] * (1.0 / acc[dv:dv + 1]))
    o_ref[...] = jnp.concatenate(outs, axis=0).T.astype(BF16)


def _attention(q, k, vt, dv, name):
    B, H, S, dq = q.shape
    _, _, nq, dva, tq = vt.shape
    return pl.pallas_call(
        functools.partial(_attn_kernel, dv=dv),
        grid=(B, nq),
        in_specs=[pl.BlockSpec((None, H, tq, dq), lambda b, i: (b, 0, i, 0)),
                  pl.BlockSpec((None, H, S, dq), lambda b, i: (b, 0, 0, 0)),
                  pl.BlockSpec((None, H, nq, dva, tq), lambda b, i: (b, 0, 0, 0, 0))],
        out_specs=pl.BlockSpec((None, tq, H * dv), lambda b, i: (b, i, 0)),
        out_shape=jax.ShapeDtypeStruct((B, S, H * dv), BF16),
        scratch_shapes=[pltpu.VMEM((H, dva, tq), F32), pltpu.VMEM((H, 8, tq), F32),
                        pltpu.VMEM((H, tq, tq), F32)],
        compiler_params=_params(("parallel", "arbitrary")),
        name=name,
    )(q, k, vt)


def _merge_kernel(x_ref, mod_ref, gpre_ref, gpost_ref, wg_ref, ya_ref, yb_ref, yc_ref,
                  wa_ref, wb_ref, wc_ref, wo_ref, o_ref):
    D = D_MODEL
    x = x_ref[...]
    h = (_rms(x, gpre_ref[...]) * (1.0 + mod_ref[1:2, :]) + mod_ref[0:1, :]).astype(BF16)
    merged = None
    for j, (y_ref, w_ref) in enumerate(((ya_ref, wa_ref), (yb_ref, wb_ref), (yc_ref, wc_ref))):
        gate = _sigmoid(jnp.dot(h, wg_ref[:, j * D:(j + 1) * D], preferred_element_type=F32))
        term = gate * jnp.dot(y_ref[...], w_ref[...], preferred_element_type=F32)
        merged = term if merged is None else merged + term
    out = _bdot(merged, wo_ref[...])
    o_ref[...] = x + mod_ref[2:3, :] * _rms(out, gpost_ref[...])


def _merge(x, mod_l, gpre, gpost, wg, ya, yb, yc, wa, wb, wc, wo, tm):
    B, S, D = x.shape
    tile = lambda w: pl.BlockSpec((None, tm, w), lambda b, i: (b, i, 0))
    return pl.pallas_call(
        _merge_kernel,
        grid=(B, S // tm),
        in_specs=[tile(D), pl.BlockSpec((None, 6, D), lambda b, i: (b, 0, 0)), _full(gpre.shape),
                  _full(gpost.shape), _full(wg.shape), tile(ya.shape[-1]), tile(yb.shape[-1]),
                  tile(yc.shape[-1]), _full(wa.shape), _full(wb.shape), _full(wc.shape),
                  _full(wo.shape)],
        out_specs=tile(D),
        out_shape=jax.ShapeDtypeStruct((B, S, D), F32),
        compiler_params=_params(("parallel", "parallel")),
        name="merge_out",
    )(x, mod_l, gpre, gpost, wg, ya, yb, yc, wa, wb, wc, wo)


def _ffn_kernel(x_ref, mod_ref, gpre_ref, gpost_ref, wu_ref, wd_ref, o_ref):
    D = D_MODEL
    x = x_ref[...]
    h = (_rms(x, gpre_ref[...]) * (1.0 + mod_ref[4:5, :]) + mod_ref[3:4, :]).astype(BF16)
    acc = None
    for j in range(D_FF // D):
        u = jnp.maximum(jnp.dot(h, wu_ref[:, j * D:(j + 1) * D], preferred_element_type=F32), 0.0)
        term = jnp.dot((u * u).astype(BF16), wd_ref[j * D:(j + 1) * D, :], preferred_element_type=F32)
        acc = term if acc is None else acc + term
    o_ref[...] = x + mod_ref[5:6, :] * _rms(acc, gpost_ref[...])


def _ffn(x, mod_l, gpre, gpost, wu, wd, tm):
    B, S, D = x.shape
    tile = pl.BlockSpec((None, tm, D), lambda b, i: (b, i, 0))
    return pl.pallas_call(
        _ffn_kernel,
        grid=(B, S // tm),
        in_specs=[tile, pl.BlockSpec((None, 6, D), lambda b, i: (b, 0, 0)), _full(gpre.shape),
                  _full(gpost.shape), _full(wu.shape), _full(wd.shape)],
        out_specs=tile,
        out_shape=jax.ShapeDtypeStruct((B, S, D), F32),
        compiler_params=_params(("parallel", "parallel")),
        name="ffn",
    )(x, mod_l, gpre, gpost, wu, wd)


def _pad_heads(w, dh, to):
    K = w.shape[0]
    return jnp.pad(w.reshape(K, N_HEADS, dh), ((0, 0), (0, 0), (0, to - dh))).reshape(K, N_HEADS * to)


def _fox_selectors():
    part = jnp.arange(3)[:, None]
    hd = jnp.arange(N_HEADS)[None, :]
    rows = (part * 128 + hd).reshape(-1)
    selq = jnp.zeros((384, 512), F32).at[rows, (hd * 128 + 64 + part).reshape(-1)].set(1.0)
    selk = jnp.zeros((384, 512), F32).at[rows, (hd * 128 + 67 + part).reshape(-1)].set(-1.0)
    lane = jnp.arange(512) % 128
    cq = ((lane >= 67) & (lane < 70)).astype(F32).reshape(1, 512)
    ck = ((lane >= 64) & (lane < 67)).astype(F32).reshape(1, 512)
    return selq.astype(BF16), selk.astype(BF16), cq, ck


def _row(v):
    return v.reshape(1, -1)


def kernel(x, c, positions, w_in, mu_shift, w0, w_decay_up, a0, w_aaa_up, w_gate_up, k_k, k_a, r_k,
           ln_x_g, ln_x_b, b_forget, q_norm_g, w_q_up, kv_norm_g, w_kv_up, w_branch_a, w_branch_b,
           w_branch_c, w_out, w_mod, b_mod, norm_mix_pre, norm_mix_post, norm_ffn_pre, norm_ffn_post,
           w_ffn_up, w_ffn_down):
    B, S, D = x.shape
    L = w_in.shape[0]
    tm = min(512, S)

    c8 = jnp.pad(c, ((0, 8 - B), (0, 0)))
    mod = _modulation(c8, w_mod, b_mod)[:, :B].reshape(L, B, 6, D)
    cos_t, sin_t = _rope_tables(positions, tm)
    selq, selk, cq, ck = _fox_selectors()

    for l in range(L):
        wi = w_in[l]
        w1 = jnp.concatenate([
            wi[:, :_FOX0],
            _pad_heads(wi[:, _FOX0:_FOX0 + 256], 64, 128) * (HEAD64 ** -0.5 * LOG2E),
            _pad_heads(wi[:, _FOX0 + 256:_FOX0 + 512], 64, 128),
            wi[:, _FOX0 + 512:_FOX0 + 768],
            jnp.pad(wi[:, _FOX0 + 768:_MLA0], ((0, 0), (0, 128 - N_HEADS))),
            wi[:, _MLA0:_MLA0 + Q_LORA_RANK + KV_LORA_RANK],
            jnp.pad(wi[:, _GATE0 - QK_ROPE_DIM:_GATE0], ((0, 0), (0, 128 - QK_ROPE_DIM))),
        ], axis=1).astype(BF16)
        bf = jnp.pad(_row(b_forget[l]), ((0, 0), (0, 128 - N_HEADS)))
        wq = (_pad_heads(w_q_up[l], QK_NOPE_DIM + QK_ROPE_DIM, 256)
              * ((QK_NOPE_DIM + QK_ROPE_DIM) ** -0.5 * LOG2E)).astype(BF16)
        wkv = w_kv_up[l].astype(BF16)

        za, fq, fk, fv, mq, mk, mv = _inproj(
            x, mod[l], _row(norm_mix_pre[l]), w1, bf, selq, selk, cq, ck, _row(q_norm_g[l]), wq,
            _row(kv_norm_g[l]), wkv, cos_t, sin_t, tm)

        zeros64 = jnp.zeros((64, RWKV_WIDTH), F32)
        ya = _rwkv(za, _row(mu_shift[l]), _row(w0[l]),
                   jnp.concatenate([w_decay_up[l], zeros64], axis=0), _row(a0[l]),
                   jnp.concatenate([zeros64, w_aaa_up[l]], axis=0), w_gate_up[l],
                   _row(k_k[l]), _row(k_a[l]), _row(r_k[l]), _row(ln_x_g[l]), _row(ln_x_b[l]), tm)
        yb = _attention(fq, fk, fv, HEAD64, "fox_attention")
        yc = _attention(mq, mk, mv, V_HEAD_DIM, "mla_attention")

        x = _merge(x, mod[l], _row(norm_mix_pre[l]), _row(norm_mix_post[l]),
                   wi[:, _GATE0:].astype(BF16), ya, yb, yc, w_branch_a[l].astype(BF16),
                   w_branch_b[l].astype(BF16), w_branch_c[l].astype(BF16), w_out[l].astype(BF16), tm)
        x = _ffn(x, mod[l], _row(norm_ffn_pre[l]), _row(norm_ffn_post[l]),
                 w_ffn_up[l].astype(BF16), w_ffn_down[l].astype(BF16), tm)
    return x
```

```python
import functools

import jax
import jax.numpy as jnp
from jax import lax
from jax.experimental import pallas as pl
from jax.experimental.pallas import tpu as pltpu

F32 = jnp.float32
BF16 = jnp.bfloat16

D_MODEL = 1024
N_HEADS = 4
HEAD64 = 64
RWKV_WIDTH = N_HEADS * HEAD64
QK_NOPE_DIM = 128
QK_ROPE_DIM = 64
V_HEAD_DIM = 128
Q_LORA_RANK = 384
KV_LORA_RANK = 256
ROPE_THETA = 10000.0
D_FF = 4 * D_MODEL
NORM_EPS = 1e-6
RWKV_LN_EPS = 64e-5
MASK_VALUE = -1e30
LOG2E = 1.4426950408889634
CHUNK = 64

_FOX0 = 1024
_MLA0 = _FOX0 + 3 * 256 + N_HEADS
_GATE0 = _MLA0 + Q_LORA_RANK + KV_LORA_RANK + QK_ROPE_DIM

_C_RWKV = 0
_C_FQ = 1024
_C_FK = 1536
_C_FV = 2048
_C_FF = 2304
_C_QL = 2432
_C_KVL = 2816
_C_KPE = 3072
_C_END = 3200

VMEM_LIMIT = 56 * 1024 * 1024


def _bdot(a, b):
    return jnp.dot(a.astype(BF16), b.astype(BF16), preferred_element_type=F32)


def _bdot_nt(a, b):
    return lax.dot_general(a.astype(BF16), b.astype(BF16), (((1,), (1,)), ((), ())),
                           preferred_element_type=F32)


def _bdot_tn(a, b):
    return lax.dot_general(a.astype(BF16), b.astype(BF16), (((0,), (0,)), ((), ())),
                           preferred_element_type=F32)


def _split2_dot(a, b_bf16):
    hi = a.astype(BF16)
    lo = (a - hi.astype(F32)).astype(BF16)
    return (jnp.dot(hi, b_bf16, preferred_element_type=F32)
            + jnp.dot(lo, b_bf16, preferred_element_type=F32))


def _split3(a):
    hi = a.astype(BF16)
    r1 = a - hi.astype(F32)
    mid = r1.astype(BF16)
    lo = (r1 - mid.astype(F32)).astype(BF16)
    return hi, mid, lo


def _rms(x, gain):
    return x * lax.rsqrt(jnp.mean(x * x, axis=-1, keepdims=True) + NORM_EPS) * gain


def _softplus(y):
    return jnp.maximum(y, 0.0) + jnp.log(1.0 + jnp.exp(-jnp.abs(y)))


def _sigmoid(y):
    return 1.0 / (1.0 + jnp.exp(-y))


def _iota(shape, dim):
    return lax.broadcasted_iota(jnp.int32, shape, dim)


def _full(shape):
    return pl.BlockSpec(shape, lambda *_: (0,) * len(shape))


def _layer(arr, l):
    rest = arr.shape[1:]
    return pl.BlockSpec((None,) + rest, lambda *_: (l,) + (0,) * len(rest))


def _params(sem):
    return pltpu.CompilerParams(dimension_semantics=sem, vmem_limit_bytes=VMEM_LIMIT)


def _mod_kernel(c_ref, w_ref, b_ref, o_ref):
    c = c_ref[...]
    o_ref[...] = _split2_dot(c * _sigmoid(c), w_ref[...].astype(BF16)) + b_ref[...]


def _modulation(c8, w_mod, b_mod):
    L, D, D6 = w_mod.shape
    return pl.pallas_call(
        _mod_kernel,
        grid=(L, D6 // D),
        in_specs=[pl.BlockSpec((8, D), lambda l, j: (0, 0)),
                  pl.BlockSpec((None, D, D), lambda l, j: (l, 0, j)),
                  pl.BlockSpec((None, 1, D), lambda l, j: (l, 0, j))],
        out_specs=pl.BlockSpec((None, 8, D), lambda l, j: (l, 0, j)),
        out_shape=jax.ShapeDtypeStruct((L, 8, D6), F32),
        compiler_params=_params(("parallel", "parallel")),
        name="adaln_mod",
    )(c8, w_mod, b_mod.reshape(L, 1, D6))


def _rope_kernel(pos_ref, inv_ref, cos_ref, sin_ref):
    ang = pos_ref[...] * inv_ref[...]
    lane = _iota(ang.shape, 1)
    cos_ref[...] = jnp.where(lane < 64, jnp.cos(ang), 0.0)
    s = jnp.sin(ang)
    sin_ref[...] = jnp.where(lane < 32, -s, jnp.where(lane < 64, s, 0.0))


def _rope_tables(positions, tm):
    B, S = positions.shape
    inv_freq = ROPE_THETA ** (-jnp.arange(0, QK_ROPE_DIM, 2, dtype=F32) / QK_ROPE_DIM)
    inv128 = jnp.tile(inv_freq, 4).reshape(1, 128)
    pos = positions.astype(F32).reshape(B, S, 1)
    out = jax.ShapeDtypeStruct((B, S, 128), F32)
    return pl.pallas_call(
        _rope_kernel,
        grid=(B, S // tm),
        in_specs=[pl.BlockSpec((None, tm, 1), lambda b, i: (b, i, 0)), _full((1, 128))],
        out_specs=[pl.BlockSpec((None, tm, 128), lambda b, i: (b, i, 0))] * 2,
        out_shape=[out, out],
        compiler_params=_params(("parallel", "parallel")),
        name="rope_tables",
    )(pos, inv128)


def _rope128(x, cos_t, sin_t):
    lane = _iota(x.shape, 1)
    partner = jnp.where(lane < 32, pltpu.roll(x, 96, 1), pltpu.roll(x, 32, 1))
    return x * cos_t + partner * sin_t


def _inproj_kernel(x_ref, mod_ref, gpre_ref, w1_ref, bf_ref, selq_ref, selk_ref, cq_ref, ck_ref,
                   qg_ref, wq_ref, kvg_ref, wkv_ref, cos_ref, sin_ref,
                   za_ref, fq_ref, fk_ref, fv_ref, mq_ref, mk_ref, mv_ref, carry_ref):
    i = pl.program_id(1)
    tm = x_ref.shape[0]

    @pl.when(i == 0)
    def _():
        carry_ref[...] = jnp.zeros_like(carry_ref)

    x = x_ref[...]
    h = (_rms(x, gpre_ref[...]) * (1.0 + mod_ref[1:2, :]) + mod_ref[0:1, :]).astype(BF16)

    za_ref[...] = jnp.dot(h, w1_ref[:, _C_RWKV:_C_FQ], preferred_element_type=F32)

    f = jnp.dot(h, w1_ref[:, _C_FF:_C_QL], preferred_element_type=F32) + bf_ref[...]
    logf = -_softplus(-f)
    tri = (_iota((tm, tm), 0) >= _iota((tm, tm), 1)).astype(BF16)
    l_hi, l_mid, l_lo = _split3(logf)
    cum = (jnp.dot(tri, l_hi, preferred_element_type=F32)
           + jnp.dot(tri, l_mid, preferred_element_type=F32)
           + jnp.dot(tri, l_lo, preferred_element_type=F32)) + carry_ref[0:1, :]
    carry_ref[0:1, :] = cum[tm - 1:tm, :]
    cc = jnp.concatenate(_split3(cum * LOG2E), axis=1)
    fq = (jnp.dot(h, w1_ref[:, _C_FQ:_C_FK], preferred_element_type=F32)
          + jnp.dot(cc, selq_ref[...], preferred_element_type=F32) + cq_ref[...])
    fk = (jnp.dot(h, w1_ref[:, _C_FK:_C_FV], preferred_element_type=F32)
          + jnp.dot(cc, selk_ref[...], preferred_element_type=F32) + ck_ref[...])
    fv_t = jnp.dot(h, w1_ref[:, _C_FV:_C_FF], preferred_element_type=F32).T
    ones_rows = (_iota((16, tm), 0) == 0).astype(BF16)
    for hd in range(N_HEADS):
        sl = slice(128 * hd, 128 * hd + 128)
        fq_ref[hd] = fq[:, sl].astype(BF16)
        fk_ref[hd] = fk[:, sl].astype(BF16)
        fv_ref[hd] = jnp.concatenate([fv_t[64 * hd:64 * hd + 64].astype(BF16), ones_rows], axis=0)

    cos_t = cos_ref[...]
    sin_t = sin_ref[...]
    q_lat = jnp.dot(h, w1_ref[:, _C_QL:_C_KVL], preferred_element_type=F32)
    q = _bdot(_rms(q_lat, qg_ref[...]), wq_ref[...])
    kv_lat = jnp.dot(h, w1_ref[:, _C_KVL:_C_KPE], preferred_element_type=F32)
    kv = _bdot(_rms(kv_lat, kvg_ref[...]), wkv_ref[...])
    k_pe = _rope128(jnp.dot(h, w1_ref[:, _C_KPE:_C_END], preferred_element_type=F32), cos_t, sin_t)
    k_pe = k_pe.astype(BF16)
    for hd in range(N_HEADS):
        o = 256 * hd
        q_pe = _rope128(q[:, o + 128:o + 256], cos_t, sin_t)
        mq_ref[hd] = jnp.concatenate([q[:, o:o + 128], q_pe], axis=1).astype(BF16)
        mk_ref[hd] = jnp.concatenate([kv[:, o:o + 128].astype(BF16), k_pe], axis=1)
        mv_ref[hd] = jnp.concatenate([kv[:, o + 128:o + 256].T.astype(BF16), ones_rows], axis=0)


def _inproj(l, x, mod, gpre, w1, bf, selq, selk, cq, ck, qg, wq, kvg, wkv, cos_t, sin_t, tm):
    B, S, D = x.shape
    H = N_HEADS
    tile = lambda w: pl.BlockSpec((None, tm, w), lambda b, i: (b, i, 0))
    head = lambda w: pl.BlockSpec((None, H, tm, w), lambda b, i: (b, 0, i, 0))
    hs = lambda w: jax.ShapeDtypeStruct((B, H, S, w), BF16)
    head_t = lambda r: pl.BlockSpec((None, H, None, r, tm), lambda b, i: (b, 0, i, 0, 0))
    hs_t = lambda r: jax.ShapeDtypeStruct((B, H, S // tm, r, tm), BF16)
    return pl.pallas_call(
        _inproj_kernel,
        grid=(B, S // tm),
        in_specs=[tile(D), pl.BlockSpec((None, None, 6, D), lambda b, i: (l, b, 0, 0)), _layer(gpre, l),
                  _layer(w1, l), _layer(bf, l), _full(selq.shape), _full(selk.shape),
                  _full(cq.shape), _full(ck.shape), _layer(qg, l), _layer(wq, l),
                  _layer(kvg, l), _layer(wkv, l), tile(128), tile(128)],
        out_specs=[tile(1024), head(128), head(128), head_t(80), head(256), head(256), head_t(144)],
        out_shape=[jax.ShapeDtypeStruct((B, S, 1024), F32), hs(128), hs(128), hs_t(80),
                   hs(256), hs(256), hs_t(144)],
        scratch_shapes=[pltpu.VMEM((8, 128), F32)],
        compiler_params=_params(("parallel", "arbitrary")),
        name="in_proj",
    )(x, mod, gpre, w1, bf, selq, selk, cq, ck, qg, wq, kvg, wkv, cos_t, sin_t)


def _rwkv_kernel(z_ref, mu_ref, w0_ref, wd_ref, a0_ref, wa_ref, wg_ref, kk_ref, ka_ref, rk_ref,
                 lng_ref, lnb_ref, y_ref,
                 prev_ref, h_ref, rt_ref, at_ref, bt_ref, kt_ref, bh_ref, kh_ref, v_ref, gam_ref,
                 ys_ref, yp_ref, gw_ref, gw2_ref, rp_ref, pw_ref):
    i = pl.program_id(1)
    tm = z_ref.shape[0]
    W = RWKV_WIDTH
    C = CHUNK

    @pl.when(i == 0)
    def _():
        prev_ref[...] = jnp.zeros_like(prev_ref)
        h_ref[...] = jnp.zeros_like(h_ref)

    z = z_ref[...]
    zprev = jnp.where(_iota((tm, 1), 0) == 0, prev_ref[0:1, :], pltpu.roll(z, 1, 0))
    prev_ref[0:1, :] = z[tm - 1:tm, :]
    zm = z + (zprev - z) * mu_ref[...]
    r = zm[:, 0:W]
    k = zm[:, W:2 * W]
    v = zm[:, 2 * W:3 * W]
    lora_in = zm[:, 3 * W:3 * W + 128]
    gd = zm[:, 3 * W + 128:3 * W + 256]
    w_log = -_softplus(-(w0_ref[...] + _bdot(jnp.tanh(lora_in), wd_ref[...]))) - 0.5
    lw = -jnp.exp(w_log)
    a = _sigmoid(a0_ref[...] + _bdot(lora_in, wa_ref[...]))
    g = _bdot(_sigmoid(gd), wg_ref[...])
    seg = ((_iota((W, W), 0) // HEAD64) == (_iota((W, W), 1) // HEAD64)).astype(BF16)
    kk = k * kk_ref[...]
    kk = kk / jnp.maximum(jnp.sqrt(_split2_dot(kk * kk, seg)), 1e-12)
    k2 = k * (1.0 + (a - 1.0) * ka_ref[...])

    tri = (_iota((C, C), 0) >= _iota((C, C), 1)).astype(BF16)
    lw3 = _split3(lw)
    cls, ces = [], []
    for c in range(tm // C):
        cl_c = sum(jnp.dot(tri, part[c * C:(c + 1) * C, :], preferred_element_type=F32) for part in lw3)
        cls.append(cl_c)
        ces.append(jnp.broadcast_to(cl_c[C - 1:C, :], (C, W)))
    cl = jnp.concatenate(cls, axis=0)
    ce = jnp.concatenate(ces, axis=0)
    e_neg = jnp.exp(-cl)
    e_end = jnp.exp(ce - cl)
    rt_ref[...] = r * jnp.exp(cl)
    at_ref[...] = -kk * jnp.exp(cl - lw)
    bt_ref[...] = kk * a * e_neg
    kt_ref[...] = k2 * e_neg
    bh_ref[...] = kk * a * e_end
    kh_ref[...] = k2 * e_end
    v_ref[...] = v
    gam_ref[...] = jnp.exp(ce)

    block_mask = (_iota((W, W), 0) // C) == (_iota((W, W), 1) // C)
    t_row = _iota((C, W), 0)
    s_col = _iota((C, W), 1) % C
    strict_lower = t_row > s_col
    incl_lower = t_row >= s_col
    eye_w = (t_row == s_col).astype(F32)

    def block(x):
        return jnp.where(block_mask, jnp.concatenate([x] * N_HEADS, axis=0), 0.0).astype(BF16)

    def fold(x_bd):
        return x_bd[0:C] + x_bd[C:2 * C] + x_bd[2 * C:3 * C] + x_bd[3 * C:4 * C]

    group = 8

    def chunk_group(gi, carry):
        rows = [pl.ds(pl.multiple_of((gi * group + c) * C, C), C) for c in range(group)]
        cs = range(group)
        a_n = [at_ref[rows[c], :] for c in cs]
        a_s = [block(a_n[c]) for c in cs]
        m = [_bdot_nt(jnp.concatenate([a_n[c], rt_ref[rows[c], :]], axis=0),
                      jnp.concatenate([block(bt_ref[rows[c], :]), block(kt_ref[rows[c], :])], axis=0))
             for c in cs]
        a_ab = [jnp.where(strict_lower, m[c][:C, :W], 0.0) for c in cs]
        m_rb = [jnp.where(incl_lower, m[c][C:, :W], 0.0) for c in cs]
        av = [_bdot(jnp.concatenate([jnp.where(strict_lower, m[c][:C, W:], 0.0),
                                     jnp.where(incl_lower, m[c][C:, W:], 0.0)], axis=0),
                    block(v_ref[rows[c], :])) for c in cs]
        t_inv = [eye_w + a_ab[c] for c in cs]
        p = a_ab
        for _ in range(5):
            p = [_bdot(p[c], block(p[c])) for c in cs]
            t_inv = [t_inv[c] + _bdot(t_inv[c], block(p[c])) for c in cs]
        x = [_bdot(t_inv[c], jnp.concatenate([a_s[c], block(av[c][:C])], axis=1)) for c in cs]
        ry = [_bdot(m_rb[c], jnp.concatenate([block(x[c][:, :W]), block(x[c][:, W:])], axis=1)) for c in cs]
        for c in cs:
            rp_ref[rows[c], :] = (rt_ref[rows[c], :] + ry[c][:, :W]).astype(BF16)
            yp_ref[rows[c], :] = av[c][C:] + ry[c][:, W:]
            bh_n = bh_ref[rows[c], :]
            p_bd = jnp.where(block_mask, _bdot_tn(bh_n, x[c][:, :W]), 0.0)
            g_bd = jnp.where(block_mask,
                             _bdot_tn(jnp.concatenate([bh_n, kh_ref[rows[c], :]], axis=0),
                                      jnp.concatenate([x[c][:, W:], v_ref[rows[c], :]], axis=0)), 0.0)
            pw_ref[rows[c], :] = fold(p_bd).astype(BF16)
            gw_ref[rows[c], :] = fold(g_bd)
            gam_row = gam_ref[pl.ds(pl.multiple_of((gi * group + c) * C, C), 1), :]
            gw2_ref[rows[c], :] = _split2_dot(eye_w * gam_row, seg)
        return carry

    lax.fori_loop(0, tm // (C * group), chunk_group, 0)

    h_w = h_ref[...]
    for c in range(tm // C):
        rows = slice(c * C, (c + 1) * C)
        out = _bdot(jnp.concatenate([rp_ref[rows, :], pw_ref[rows, :]], axis=0), block(h_w))
        ys_ref[rows, :] = out[:C] + yp_ref[rows, :]
        h_w = gw2_ref[rows, :] * h_w + out[C:] + gw_ref[rows, :]
    h_ref[...] = h_w

    y = ys_ref[...]
    inv = 1.0 / HEAD64
    mean = _split2_dot(y, seg) * inv
    d = y - mean
    var = _split2_dot(d * d, seg) * inv
    yn = d * lax.rsqrt(var + RWKV_LN_EPS) * lng_ref[...] + lnb_ref[...]
    bonus = _split2_dot(r * k2 * rk_ref[...], seg) * v
    y_ref[...] = ((yn + bonus) * g).astype(BF16)


def _rwkv(l, za, mu, w0, wd, a0, wa, wg, k_k, k_a, r_k, ln_g, ln_b, tm):
    B, S, _ = za.shape
    W = RWKV_WIDTH
    smalls = [mu, w0, wd, a0, wa, wg, k_k, k_a, r_k, ln_g, ln_b]
    return pl.pallas_call(
        _rwkv_kernel,
        grid=(B, S // tm),
        in_specs=[pl.BlockSpec((None, tm, 1024), lambda b, i: (b, i, 0))] + [_layer(s, l) for s in smalls],
        out_specs=pl.BlockSpec((None, tm, W), lambda b, i: (b, i, 0)),
        out_shape=jax.ShapeDtypeStruct((B, S, W), BF16),
        scratch_shapes=[pltpu.VMEM((8, 1024), F32), pltpu.VMEM((CHUNK, W), F32)]
                       + [pltpu.VMEM((tm, W), F32)] * 12 + [pltpu.VMEM((tm, W), BF16)] * 2,
        compiler_params=_params(("parallel", "arbitrary")),
        name="rwkv7_chunked",
    )(za, *smalls)


def _attn_kernel(q_ref, k_ref, vt_ref, o_ref, acc_ref, m_ref, s_ref, *, dv):
    qi = pl.program_id(1)
    H, tq, _ = q_ref.shape
    tk = vt_ref.shape[-1]
    acc_ref[...] = jnp.zeros_like(acc_ref)
    m_ref[...] = jnp.full_like(m_ref, MASK_VALUE)

    def step(j, q0, qn, masked):
        rows = pl.ds(pl.multiple_of(j * tk, tk), tk)
        lanes = slice(q0, q0 + qn)
        if masked:
            key_le_query = _iota((tk, qn), 0) <= _iota((tk, qn), 1)
        m_news, alphas = [], []
        for hd in range(H):
            st = _bdot_nt(k_ref[hd, rows, :], q_ref[hd, lanes, :])
            if masked:
                st = jnp.where(key_le_query, st, MASK_VALUE)
            s_ref[hd, :, lanes] = st
            m_old = m_ref[hd, 0:1, lanes]
            m_new = jnp.maximum(m_old, jnp.max(st, axis=0, keepdims=True))
            m_ref[hd, 0:1, lanes] = m_new
            m_news.append(m_new)
            alphas.append(jnp.exp2(m_old - m_new))
        for hd in range(H):
            p = jnp.exp2(s_ref[hd, :, lanes] - m_news[hd]).astype(BF16)
            acc_ref[hd, :, lanes] = (alphas[hd] * acc_ref[hd, :, lanes]
                                     + jnp.dot(vt_ref[hd, j], p, preferred_element_type=F32))

    def body(j, carry):
        step(j, 0, tq, False)
        return carry

    first_diag = qi * (tq // tk)
    lax.fori_loop(0, first_diag, body, 0)
    step(first_diag, 0, tq, True)
    step(first_diag + 1, tk, tq - tk, True)
    outs = []
    for hd in range(H):
        acc = acc_ref[hd]
        outs.append(acc[:dv] * (1.0 / acc[dv:dv + 1]))
    o_ref[...] = jnp.concatenate(outs, axis=0).T.astype(BF16)


def _attention(q, k, vt, dv, name):
    B, H, S, dq = q.shape
    _, _, _, dva, tk = vt.shape
    tq = 2 * tk
    return pl.pallas_call(
        functools.partial(_attn_kernel, dv=dv),
        grid=(B, S // tq),
        in_specs=[pl.BlockSpec((None, H, tq, dq), lambda b, i: (b, 0, i, 0)),
                  pl.BlockSpec((None, H, S, dq), lambda b, i: (b, 0, 0, 0)),
                  pl.BlockSpec((None, H, S // tk, dva, tk), lambda b, i: (b, 0, 0, 0, 0))],
        out_specs=pl.BlockSpec((None, tq, H * dv), lambda b, i: (b, i, 0)),
        out_shape=jax.ShapeDtypeStruct((B, S, H * dv), BF16),
        scratch_shapes=[pltpu.VMEM((H, dva, tq), F32), pltpu.VMEM((H, 8, tq), F32),
                        pltpu.VMEM((H, tk, tq), F32)],
        compiler_params=_params(("parallel", "arbitrary")),
        name=name,
    )(q, k, vt)


def _merge_kernel(x_ref, mod_ref, gpre_ref, gpost_ref, wg_ref, ya_ref, yb_ref, yc_ref,
                  wa_ref, wb_ref, wc_ref, wo_ref, o_ref):
    D = D_MODEL
    x = x_ref[...]
    h = (_rms(x, gpre_ref[...]) * (1.0 + mod_ref[1:2, :]) + mod_ref[0:1, :]).astype(BF16)
    merged = None
    for j, (y_ref, w_ref) in enumerate(((ya_ref, wa_ref), (yb_ref, wb_ref), (yc_ref, wc_ref))):
        gate = _sigmoid(jnp.dot(h, wg_ref[:, j * D:(j + 1) * D], preferred_element_type=F32))
        term = gate * jnp.dot(y_ref[...], w_ref[...], preferred_element_type=F32)
        merged = term if merged is None else merged + term
    out = _bdot(merged, wo_ref[...])
    o_ref[...] = x + mod_ref[2:3, :] * _rms(out, gpost_ref[...])


def _merge(l, x, mod, gpre, gpost, wg, ya, yb, yc, wa, wb, wc, wo, tm):
    B, S, D = x.shape
    tile = lambda w: pl.BlockSpec((None, tm, w), lambda b, i: (b, i, 0))
    return pl.pallas_call(
        _merge_kernel,
        grid=(B, S // tm),
        in_specs=[tile(D), pl.BlockSpec((None, None, 6, D), lambda b, i: (l, b, 0, 0)), _layer(gpre, l),
                  _layer(gpost, l), _layer(wg, l), tile(ya.shape[-1]), tile(yb.shape[-1]),
                  tile(yc.shape[-1]), _layer(wa, l), _layer(wb, l), _layer(wc, l),
                  _layer(wo, l)],
        out_specs=tile(D),
        out_shape=jax.ShapeDtypeStruct((B, S, D), F32),
        compiler_params=_params(("parallel", "parallel")),
        name="merge_out",
    )(x, mod, gpre, gpost, wg, ya, yb, yc, wa, wb, wc, wo)


def _ffn_kernel(x_ref, mod_ref, gpre_ref, gpost_ref, wu_ref, wd_ref, o_ref):
    D = D_MODEL
    x = x_ref[...]
    h = (_rms(x, gpre_ref[...]) * (1.0 + mod_ref[4:5, :]) + mod_ref[3:4, :]).astype(BF16)
    acc = None
    for j in range(D_FF // D):
        u = jnp.maximum(jnp.dot(h, wu_ref[:, j * D:(j + 1) * D], preferred_element_type=F32), 0.0)
        term = jnp.dot((u * u).astype(BF16), wd_ref[j * D:(j + 1) * D, :], preferred_element_type=F32)
        acc = term if acc is None else acc + term
    o_ref[...] = x + mod_ref[5:6, :] * _rms(acc, gpost_ref[...])


def _ffn(l, x, mod, gpre, gpost, wu, wd, tm):
    B, S, D = x.shape
    tile = pl.BlockSpec((None, tm, D), lambda b, i: (b, i, 0))
    return pl.pallas_call(
        _ffn_kernel,
        grid=(B, S // tm),
        in_specs=[tile, pl.BlockSpec((None, None, 6, D), lambda b, i: (l, b, 0, 0)), _layer(gpre, l),
                  _layer(gpost, l), _layer(wu, l), _layer(wd, l)],
        out_specs=tile,
        out_shape=jax.ShapeDtypeStruct((B, S, D), F32),
        compiler_params=_params(("parallel", "parallel")),
        name="ffn",
    )(x, mod, gpre, gpost, wu, wd)


def _pad_heads(w, dh, to):
    L, K, _ = w.shape
    return jnp.pad(w.reshape(L, K, N_HEADS, dh),
                   ((0, 0), (0, 0), (0, 0), (0, to - dh))).reshape(L, K, N_HEADS * to)


def _fox_selectors():
    row = jnp.arange(384)[:, None]
    col = jnp.arange(512)[None, :]
    part, hd = row // 128, row % 128
    valid = hd < N_HEADS
    selq = jnp.where(valid & (col == hd * 128 + 64 + part), 1.0, 0.0)
    selk = jnp.where(valid & (col == hd * 128 + 67 + part), -1.0, 0.0)
    lane = jnp.arange(512) % 128
    cq = ((lane >= 67) & (lane < 70)).astype(F32).reshape(1, 512)
    ck = ((lane >= 64) & (lane < 67)).astype(F32).reshape(1, 512)
    return selq.astype(BF16), selk.astype(BF16), cq, ck


def _rows(v):
    return v.reshape(v.shape[0], 1, -1)


def kernel(x, c, positions, w_in, mu_shift, w0, w_decay_up, a0, w_aaa_up, w_gate_up, k_k, k_a, r_k,
           ln_x_g, ln_x_b, b_forget, q_norm_g, w_q_up, kv_norm_g, w_kv_up, w_branch_a, w_branch_b,
           w_branch_c, w_out, w_mod, b_mod, norm_mix_pre, norm_mix_post, norm_ffn_pre, norm_ffn_post,
           w_ffn_up, w_ffn_down):
    B, S, D = x.shape
    L = w_in.shape[0]
    tm = min(512, S)
    bf16 = lambda w: w.astype(BF16)

    c8 = jnp.pad(c, ((0, 8 - B), (0, 0)))
    mod = _modulation(c8, w_mod, b_mod)[:, :B].reshape(L, B, 6, D)
    cos_t, sin_t = _rope_tables(positions, tm)
    selq, selk, cq, ck = _fox_selectors()

    fq_scale = HEAD64 ** -0.5 * LOG2E
    mq_scale = (QK_NOPE_DIM + QK_ROPE_DIM) ** -0.5 * LOG2E
    w1 = jnp.concatenate([
        bf16(w_in[:, :, :_FOX0]),
        _pad_heads(bf16(w_in[:, :, _FOX0:_FOX0 + 256] * fq_scale), 64, 128),
        _pad_heads(bf16(w_in[:, :, _FOX0 + 256:_FOX0 + 512]), 64, 128),
        bf16(w_in[:, :, _FOX0 + 512:_FOX0 + 768]),
        jnp.pad(bf16(w_in[:, :, _FOX0 + 768:_MLA0]), ((0, 0), (0, 0), (0, 128 - N_HEADS))),
        bf16(w_in[:, :, _MLA0:_MLA0 + Q_LORA_RANK + KV_LORA_RANK]),
        jnp.pad(bf16(w_in[:, :, _GATE0 - QK_ROPE_DIM:_GATE0]), ((0, 0), (0, 0), (0, 128 - QK_ROPE_DIM))),
    ], axis=-1)
    w_gates = bf16(w_in[:, :, _GATE0:])
    bf = jnp.pad(_rows(b_forget), ((0, 0), (0, 0), (0, 128 - N_HEADS)))
    wq = _pad_heads(bf16(w_q_up * mq_scale), QK_NOPE_DIM + QK_ROPE_DIM, 256)
    wkv = bf16(w_kv_up)
    zeros64 = jnp.zeros((L, 64, RWKV_WIDTH), F32)
    wd = jnp.concatenate([w_decay_up, zeros64], axis=1)
    wa = jnp.concatenate([zeros64, w_aaa_up], axis=1)
    wba, wbb, wbc, wo = bf16(w_branch_a), bf16(w_branch_b), bf16(w_branch_c), bf16(w_out)
    wu, wdn = bf16(w_ffn_up), bf16(w_ffn_down)
    g_mix_pre, g_mix_post = _rows(norm_mix_pre), _rows(norm_mix_post)
    g_ffn_pre, g_ffn_post = _rows(norm_ffn_pre), _rows(norm_ffn_post)
    rwkv_rows = [_rows(v) for v in (mu_shift, w0)] + [wd, _rows(a0), wa, w_gate_up] + [
        _rows(v) for v in (k_k, k_a, r_k.reshape(L, -1), ln_x_g, ln_x_b)]
    qg, kvg = _rows(q_norm_g), _rows(kv_norm_g)

    for l in range(L):
        za, fq, fk, fv, mq, mk, mv = _inproj(l, x, mod, g_mix_pre, w1, bf, selq, selk, cq, ck, qg, wq,
                                             kvg, wkv, cos_t, sin_t, tm)
        ya = _rwkv(l, za, *rwkv_rows, tm)
        yb = _attention(fq, fk, fv, HEAD64, "fox_attention")
        yc = _attention(mq, mk, mv, V_HEAD_DIM, "mla_attention")
        x = _merge(l, x, mod, g_mix_pre, g_mix_post, w_gates, ya, yb, yc, wba, wbb, wbc, wo, tm)
        x = _ffn(l, x, mod, g_ffn_pre, g_ffn_post, wu, wdn, tm)
    return x
```

```python
import functools

import jax
import jax.numpy as jnp
from jax import lax
from jax.experimental import pallas as pl
from jax.experimental.pallas import tpu as pltpu

F32 = jnp.float32
BF16 = jnp.bfloat16

D_MODEL = 1024
N_HEADS = 4
HEAD64 = 64
RWKV_WIDTH = N_HEADS * HEAD64
QK_NOPE_DIM = 128
QK_ROPE_DIM = 64
V_HEAD_DIM = 128
Q_LORA_RANK = 384
KV_LORA_RANK = 256
ROPE_THETA = 10000.0
D_FF = 4 * D_MODEL
NORM_EPS = 1e-6
RWKV_LN_EPS = 64e-5
MASK_VALUE = -1e30
LOG2E = 1.4426950408889634
EXP_NEG_HALF = 0.6065306597126334
CHUNK = 64

_FOX0 = 1024
_MLA0 = _FOX0 + 3 * 256 + N_HEADS
_GATE0 = _MLA0 + Q_LORA_RANK + KV_LORA_RANK + QK_ROPE_DIM

_C_RWKV = 0
_C_FQ = 1024
_C_FK = 1280
_C_FV = 1536
_C_FF = 1792
_C_QL = 1920
_C_KVL = 2304
_C_KPE = 2560
_C_END = 2688

VMEM_LIMIT = 56 * 1024 * 1024


def _bdot(a, b):
    return jnp.dot(a.astype(BF16), b.astype(BF16), preferred_element_type=F32)


def _bdot_nt(a, b):
    return lax.dot_general(a.astype(BF16), b.astype(BF16), (((1,), (1,)), ((), ())),
                           preferred_element_type=F32)


def _bdot_tn(a, b):
    return lax.dot_general(a.astype(BF16), b.astype(BF16), (((0,), (0,)), ((), ())),
                           preferred_element_type=F32)


def _split2_dot(a, b_bf16):
    hi = a.astype(BF16)
    lo = (a - hi.astype(F32)).astype(BF16)
    return (jnp.dot(hi, b_bf16, preferred_element_type=F32)
            + jnp.dot(lo, b_bf16, preferred_element_type=F32))


def _split3(a):
    hi = a.astype(BF16)
    r1 = a - hi.astype(F32)
    mid = r1.astype(BF16)
    lo = (r1 - mid.astype(F32)).astype(BF16)
    return hi, mid, lo


def _rms(x, gain):
    return x * lax.rsqrt(jnp.mean(x * x, axis=-1, keepdims=True) + NORM_EPS) * gain


def _softplus(y):
    return jnp.maximum(y, 0.0) + jnp.log(1.0 + jnp.exp(-jnp.abs(y)))


def _sigmoid(y):
    return 1.0 / (1.0 + jnp.exp(-y))


def _iota(shape, dim):
    return lax.broadcasted_iota(jnp.int32, shape, dim)


def _full(shape):
    return pl.BlockSpec(shape, lambda *_: (0,) * len(shape))


def _layer(arr, l):
    rest = arr.shape[1:]
    return pl.BlockSpec((None,) + rest, lambda *_: (l,) + (0,) * len(rest))


def _params(sem):
    return pltpu.CompilerParams(dimension_semantics=sem, vmem_limit_bytes=VMEM_LIMIT)


def _mod_kernel(c_ref, w_ref, b_ref, o_ref):
    c = c_ref[...]
    o_ref[...] = _split2_dot(c * _sigmoid(c), w_ref[...].astype(BF16)) + b_ref[...]


def _modulation(c8, w_mod, b_mod):
    L, D, D6 = w_mod.shape
    return pl.pallas_call(
        _mod_kernel,
        grid=(L, D6 // D),
        in_specs=[pl.BlockSpec((8, D), lambda l, j: (0, 0)),
                  pl.BlockSpec((None, D, D), lambda l, j: (l, 0, j)),
                  pl.BlockSpec((None, 1, D), lambda l, j: (l, 0, j))],
        out_specs=pl.BlockSpec((None, 8, D), lambda l, j: (l, 0, j)),
        out_shape=jax.ShapeDtypeStruct((L, 8, D6), F32),
        compiler_params=_params(("parallel", "parallel")),
        name="adaln_mod",
    )(c8, w_mod, b_mod.reshape(L, 1, D6))


def _rope_kernel(pos_ref, inv_ref, cos_ref, sin_ref):
    ang = pos_ref[...] * inv_ref[...]
    lane = _iota(ang.shape, 1)
    cos_ref[...] = jnp.where(lane < 64, jnp.cos(ang), 0.0)
    s = jnp.sin(ang)
    sin_ref[...] = jnp.where(lane < 32, -s, jnp.where(lane < 64, s, 0.0))


def _rope_tables(positions, tm):
    B, S = positions.shape
    inv_freq = ROPE_THETA ** (-jnp.arange(0, QK_ROPE_DIM, 2, dtype=F32) / QK_ROPE_DIM)
    inv128 = jnp.tile(inv_freq, 4).reshape(1, 128)
    pos = positions.astype(F32).reshape(B, S, 1)
    out = jax.ShapeDtypeStruct((B, S, 128), F32)
    return pl.pallas_call(
        _rope_kernel,
        grid=(B, S // tm),
        in_specs=[pl.BlockSpec((None, tm, 1), lambda b, i: (b, i, 0)), _full((1, 128))],
        out_specs=[pl.BlockSpec((None, tm, 128), lambda b, i: (b, i, 0))] * 2,
        out_shape=[out, out],
        compiler_params=_params(("parallel", "parallel")),
        name="rope_tables",
    )(pos, inv128)


def _rope128(x, cos_t, sin_t):
    lane = _iota(x.shape, 1)
    partner = jnp.where(lane < 32, pltpu.roll(x, 96, 1), pltpu.roll(x, 32, 1))
    return x * cos_t + partner * sin_t


def _inproj_kernel(x_ref, mod_ref, gpre_ref, w1_ref, bf_ref, sel_ref,
                   qg_ref, wq_ref, kvg_ref, wkv_ref, cos_ref, sin_ref,
                   za_ref, fq_ref, fk_ref, fv_ref, mq_ref, mk_ref, mv_ref, carry_ref):
    i = pl.program_id(1)
    tm = x_ref.shape[0]

    @pl.when(i == 0)
    def _():
        carry_ref[...] = jnp.zeros_like(carry_ref)

    x = x_ref[...]
    h = (_rms(x, gpre_ref[...]) * (1.0 + mod_ref[1:2, :]) + mod_ref[0:1, :]).astype(BF16)

    za_ref[...] = jnp.dot(h, w1_ref[:, _C_RWKV:_C_FQ], preferred_element_type=F32)

    f = jnp.dot(h, w1_ref[:, _C_FF:_C_QL], preferred_element_type=F32) + bf_ref[...]
    logf = -_softplus(-f)
    tri = (_iota((tm, tm), 0) >= _iota((tm, tm), 1)).astype(BF16)
    l_hi, l_mid, l_lo = _split3(logf)
    cum = (jnp.dot(tri, l_hi, preferred_element_type=F32)
           + jnp.dot(tri, l_mid, preferred_element_type=F32)
           + jnp.dot(tri, l_lo, preferred_element_type=F32)) + carry_ref[0:1, :]
    carry_ref[0:1, :] = cum[tm - 1:tm, :]
    cc = jnp.concatenate(_split3(cum * LOG2E), axis=1)
    aug = jnp.dot(cc, sel_ref[...], preferred_element_type=F32)
    fqk = jnp.dot(h, w1_ref[:, _C_FQ:_C_FV], preferred_element_type=F32)
    lane = _iota((1, 128), 1)
    q_ones = (lane >= 67) & (lane < 70)
    k_ones = (lane >= 64) & (lane < 67)
    fv_t = jnp.dot(h, w1_ref[:, _C_FV:_C_FF], preferred_element_type=F32).T
    ones_rows = (_iota((16, tm), 0) == 0).astype(BF16)
    for hd in range(N_HEADS):
        aug_h = aug[:, 128 * hd:128 * hd + 128]
        q_grp = fqk[:, 128 * (hd // 2):128 * (hd // 2) + 128]
        k_grp = fqk[:, 256 + 128 * (hd // 2):256 + 128 * (hd // 2) + 128]
        if hd % 2:
            q_grp = pltpu.roll(q_grp, 64, 1)
            k_grp = pltpu.roll(k_grp, 64, 1)
        fq_ref[hd] = jnp.where(lane < 64, q_grp, jnp.where(q_ones, 1.0, aug_h)).astype(BF16)
        fk_ref[hd] = jnp.where(lane < 64, k_grp, jnp.where(k_ones, 1.0, aug_h)).astype(BF16)
        fv_ref[hd] = jnp.concatenate([fv_t[64 * hd:64 * hd + 64].astype(BF16), ones_rows], axis=0)

    cos_t = cos_ref[...]
    sin_t = sin_ref[...]
    q_lat = jnp.dot(h, w1_ref[:, _C_QL:_C_KVL], preferred_element_type=F32)
    q = _bdot(_rms(q_lat, qg_ref[...]), wq_ref[...])
    kv_lat = jnp.dot(h, w1_ref[:, _C_KVL:_C_KPE], preferred_element_type=F32)
    kv = _bdot(_rms(kv_lat, kvg_ref[...]), wkv_ref[...])
    k_pe = _rope128(jnp.dot(h, w1_ref[:, _C_KPE:_C_END], preferred_element_type=F32), cos_t, sin_t)
    k_pe = k_pe.astype(BF16)
    for hd in range(N_HEADS):
        o = 256 * hd
        q_pe = _rope128(q[:, o + 128:o + 256], cos_t, sin_t)
        mq_ref[hd] = jnp.concatenate([q[:, o:o + 128], q_pe], axis=1).astype(BF16)
        mk_ref[hd] = jnp.concatenate([kv[:, o:o + 128].astype(BF16), k_pe], axis=1)
        mv_ref[hd] = jnp.concatenate([kv[:, o + 128:o + 256].T.astype(BF16), ones_rows], axis=0)


def _inproj(l, x, mod, gpre, w1, bf, sel, qg, wq, kvg, wkv, cos_t, sin_t, tm):
    B, S, D = x.shape
    H = N_HEADS
    tile = lambda w: pl.BlockSpec((None, tm, w), lambda b, i: (b, i, 0))
    head = lambda w: pl.BlockSpec((None, H, tm, w), lambda b, i: (b, 0, i, 0))
    hs = lambda w: jax.ShapeDtypeStruct((B, H, S, w), BF16)
    head_t = lambda r: pl.BlockSpec((None, H, None, r, tm), lambda b, i: (b, 0, i, 0, 0))
    hs_t = lambda r: jax.ShapeDtypeStruct((B, H, S // tm, r, tm), BF16)
    return pl.pallas_call(
        _inproj_kernel,
        grid=(B, S // tm),
        in_specs=[tile(D), pl.BlockSpec((None, None, 6, D), lambda b, i: (l, b, 0, 0)), _layer(gpre, l),
                  _layer(w1, l), _layer(bf, l), _full(sel.shape), _layer(qg, l), _layer(wq, l),
                  _layer(kvg, l), _layer(wkv, l), tile(128), tile(128)],
        out_specs=[tile(1024), head(128), head(128), head_t(80), head(256), head(256), head_t(144)],
        out_shape=[jax.ShapeDtypeStruct((B, S, 1024), F32), hs(128), hs(128), hs_t(80),
                   hs(256), hs(256), hs_t(144)],
        scratch_shapes=[pltpu.VMEM((8, 128), F32)],
        compiler_params=_params(("parallel", "arbitrary")),
        name="in_proj",
    )(x, mod, gpre, w1, bf, sel, qg, wq, kvg, wkv, cos_t, sin_t)


def _rwkv_kernel(z_ref, mu_ref, w0_ref, wd_ref, a0_ref, wa_ref, wg_ref, kk_ref, ka_ref, rk_ref,
                 lng_ref, lnb_ref, y_ref,
                 prev_ref, h_ref, rt_ref, at_ref, bt_ref, kt_ref, bh_ref, kh_ref, v_ref, gam_ref,
                 ys_ref, yp_ref, gw_ref, gw2_ref, pw_ref, rp_ref):
    i = pl.program_id(1)
    tm = z_ref.shape[0]
    W = RWKV_WIDTH
    C = CHUNK

    @pl.when(i == 0)
    def _():
        prev_ref[...] = jnp.zeros_like(prev_ref)
        h_ref[...] = jnp.zeros_like(h_ref)

    z = z_ref[...]
    zroll = pltpu.roll(z, 1, 0)
    zprev = jnp.concatenate([jnp.where(_iota((8, 1), 0) == 0, prev_ref[0:1, :], zroll[0:8]), zroll[8:]], axis=0)
    prev_ref[0:1, :] = z[tm - 1:tm, :]
    zm = z + (zprev - z) * mu_ref[...]
    r = zm[:, 0:W]
    k = zm[:, W:2 * W]
    v = zm[:, 2 * W:3 * W]
    lora_in = zm[:, 3 * W:3 * W + 128]
    gd = zm[:, 3 * W + 128:3 * W + 256]
    lw = -EXP_NEG_HALF * _sigmoid(w0_ref[...] + _bdot(jnp.tanh(lora_in), wd_ref[...]))
    a = _sigmoid(a0_ref[...] + _bdot(lora_in, wa_ref[...]))
    g = _bdot(_sigmoid(gd), wg_ref[...])
    seg = ((_iota((W, W), 0) // HEAD64) == (_iota((W, W), 1) // HEAD64)).astype(BF16)
    kk = k * kk_ref[...]
    kk = kk * lax.rsqrt(jnp.maximum(_split2_dot(kk * kk, seg), 1e-24))
    k2 = k * (1.0 + (a - 1.0) * ka_ref[...])

    tri = (_iota((C, C), 0) >= _iota((C, C), 1)).astype(BF16)
    lw3 = _split3(lw)
    cls, ces = [], []
    for c in range(tm // C):
        cl_c = sum(jnp.dot(tri, part[c * C:(c + 1) * C, :], preferred_element_type=F32) for part in lw3)
        cls.append(cl_c)
        ces.append(jnp.broadcast_to(cl_c[C - 1:C, :], (C, W)))
    cl = jnp.concatenate(cls, axis=0)
    ce = jnp.concatenate(ces, axis=0)
    e_neg = jnp.exp(-cl)
    e_end = jnp.exp(ce - cl)
    rt_ref[...] = r * jnp.exp(cl)
    at_ref[...] = -kk * jnp.exp(cl - lw)
    bt_ref[...] = kk * a * e_neg
    kt_ref[...] = k2 * e_neg
    bh_ref[...] = kk * a * e_end
    kh_ref[...] = k2 * e_end
    v_ref[...] = v
    gam_ref[...] = jnp.exp(ce)

    block_mask = (_iota((W, W), 0) // C) == (_iota((W, W), 1) // C)
    t_row = _iota((C, W), 0)
    s_col = _iota((C, W), 1) % C
    strict_lower = t_row > s_col
    incl_lower = t_row >= s_col
    eye_w = (t_row == s_col).astype(F32)

    def block(x):
        return jnp.where(block_mask, jnp.concatenate([x] * N_HEADS, axis=0), 0.0).astype(BF16)

    def fold(x_bd):
        return x_bd[0:C] + x_bd[C:2 * C] + x_bd[2 * C:3 * C] + x_bd[3 * C:4 * C]

    group = 8

    def chunk_group(gi, carry):
        rows = [pl.ds(pl.multiple_of((gi * group + c) * C, C), C) for c in range(group)]
        cs = range(group)
        a_n = [at_ref[rows[c], :] for c in cs]
        a_s = [block(a_n[c]) for c in cs]
        m = [_bdot_nt(jnp.concatenate([a_n[c], rt_ref[rows[c], :]], axis=0),
                      jnp.concatenate([block(bt_ref[rows[c], :]), block(kt_ref[rows[c], :])], axis=0))
             for c in cs]
        a_ab = [jnp.where(strict_lower, m[c][:C, :W], 0.0) for c in cs]
        m_rb = [jnp.where(incl_lower, m[c][C:, :W], 0.0) for c in cs]
        av = [_bdot(jnp.concatenate([jnp.where(strict_lower, m[c][:C, W:], 0.0),
                                     jnp.where(incl_lower, m[c][C:, W:], 0.0)], axis=0),
                    block(v_ref[rows[c], :])) for c in cs]
        t_inv = [eye_w + a_ab[c] for c in cs]
        p = a_ab
        for _ in range(5):
            p = [_bdot(p[c], block(p[c])) for c in cs]
            t_inv = [t_inv[c] + _bdot(t_inv[c], block(p[c])) for c in cs]
        x = [_bdot(t_inv[c], jnp.concatenate([a_s[c], block(av[c][:C])], axis=1)) for c in cs]
        ry = [_bdot(m_rb[c], jnp.concatenate([block(x[c][:, :W]), block(x[c][:, W:])], axis=1)) for c in cs]
        for c in cs:
            rp_ref[rows[c], :] = (rt_ref[rows[c], :] + ry[c][:, :W]).astype(BF16)
            yp_ref[rows[c], :] = av[c][C:] + ry[c][:, W:]
            bh_n = bh_ref[rows[c], :]
            p_bd = jnp.where(block_mask, _bdot_tn(bh_n, x[c][:, :W]), 0.0)
            g_bd = jnp.where(block_mask,
                             _bdot_tn(jnp.concatenate([bh_n, kh_ref[rows[c], :]], axis=0),
                                      jnp.concatenate([x[c][:, W:], v_ref[rows[c], :]], axis=0)), 0.0)
            pw_ref[rows[c], :] = fold(p_bd)
            gw_ref[rows[c], :] = fold(g_bd)
            gam_row = gam_ref[pl.ds(pl.multiple_of((gi * group + c) * C, C), 1), :]
            gw2_ref[rows[c], :] = _split2_dot(eye_w * gam_row, seg)
        return carry

    lax.fori_loop(0, tm // (C * group), chunk_group, 0)

    n_chunks = tm // C

    def chunk_map(c):
        rows = slice(c * C, (c + 1) * C)
        return gw2_ref[rows, :], pw_ref[rows, :], gw_ref[rows, :], gam_ref[c * C:c * C + 1, :]

    def compose(first, second):
        d1, p1, g1, row1 = first
        d2, p2, g2, row2 = second
        cross = _bdot(p2, jnp.concatenate([block(p1), block(g1)], axis=1))
        return (d2 * d1, d2 * p1 + p2 * row1 + cross[:, :W], d2 * g1 + cross[:, W:] + g2, row2 * row1)

    def apply(mp, h, ph):
        return mp[0] * h + ph + mp[2]

    singles = [chunk_map(c) for c in range(n_chunks)]
    pairs = [compose(singles[2 * k], singles[2 * k + 1]) for k in range(n_chunks // 2)]
    quads = [compose(pairs[2 * k], pairs[2 * k + 1]) for k in range(n_chunks // 4)]
    starts = {0: h_ref[...]}
    for c in range(n_chunks):
        rows = slice(c * C, (c + 1) * C)
        spans = [(1, singles[c])]
        if c % 2 == 0:
            spans.append((2, pairs[c // 2]))
        if c % 4 == 0:
            spans.append((4, quads[c // 4]))
        spans = [(n, mp) for n, mp in spans if c + n not in starts]
        lhs = jnp.concatenate([rp_ref[rows, :]] + [mp[1].astype(BF16) for _, mp in spans], axis=0)
        out = _bdot(lhs, block(starts[c]))
        ys_ref[rows, :] = out[:C] + yp_ref[rows, :]
        for k, (n, mp) in enumerate(spans):
            starts[c + n] = apply(mp, starts[c], out[(k + 1) * C:(k + 2) * C])
    h_ref[...] = starts[n_chunks]

    y = ys_ref[...]
    inv = 1.0 / HEAD64
    mean = _split2_dot(y, seg) * inv
    d = y - mean
    var = _split2_dot(d * d, seg) * inv
    yn = d * lax.rsqrt(var + RWKV_LN_EPS) * lng_ref[...] + lnb_ref[...]
    bonus = _split2_dot(r * k2 * rk_ref[...], seg) * v
    y_ref[...] = ((yn + bonus) * g).astype(BF16)


def _rwkv(l, za, mu, w0, wd, a0, wa, wg, k_k, k_a, r_k, ln_g, ln_b, tm):
    B, S, _ = za.shape
    W = RWKV_WIDTH
    smalls = [mu, w0, wd, a0, wa, wg, k_k, k_a, r_k, ln_g, ln_b]
    return pl.pallas_call(
        _rwkv_kernel,
        grid=(B, S // tm),
        in_specs=[pl.BlockSpec((None, tm, 1024), lambda b, i: (b, i, 0))] + [_layer(s, l) for s in smalls],
        out_specs=pl.BlockSpec((None, tm, W), lambda b, i: (b, i, 0)),
        out_shape=jax.ShapeDtypeStruct((B, S, W), BF16),
        scratch_shapes=[pltpu.VMEM((8, 1024), F32), pltpu.VMEM((CHUNK, W), F32)]
                       + [pltpu.VMEM((tm, W), F32)] * 13 + [pltpu.VMEM((tm, W), BF16)],
        compiler_params=_params(("parallel", "arbitrary")),
        name="rwkv7_chunked",
    )(za, *smalls)


def _attn_kernel(q_ref, k_ref, vt_ref, o_ref, acc_ref, m_ref, s_ref, *, dv):
    qi = pl.program_id(1)
    H, tq, _ = q_ref.shape
    tk = vt_ref.shape[-1]
    acc_ref[...] = jnp.zeros_like(acc_ref)
    m_ref[...] = jnp.full_like(m_ref, MASK_VALUE)

    def step(j, q0, qn, masked):
        rows = pl.ds(pl.multiple_of(j * tk, tk), tk)
        lanes = slice(q0, q0 + qn)
        if masked:
            key_le_query = _iota((tk, qn), 0) <= _iota((tk, qn), 1)
        m_news, alphas = [], []
        for hd in range(H):
            st = _bdot_nt(k_ref[hd, rows, :], q_ref[hd, lanes, :])
            if masked:
                st = jnp.where(key_le_query, st, MASK_VALUE)
            s_ref[hd, :, lanes] = st
            m_old = m_ref[hd, 0:1, lanes]
            m_new = jnp.maximum(m_old, jnp.max(st, axis=0, keepdims=True))
            m_ref[hd, 0:1, lanes] = m_new
            m_news.append(m_new)
            alphas.append(jnp.exp2(m_old - m_new))
        for hd in range(H):
            p = jnp.exp2(s_ref[hd, :, lanes] - m_news[hd]).astype(BF16)
            acc_ref[hd, :, lanes] = (alphas[hd] * acc_ref[hd, :, lanes]
                                     + jnp.dot(vt_ref[hd, j], p, preferred_element_type=F32))

    def body(j, carry):
        step(j, 0, tq, False)
        return carry

    first_diag = qi * (tq // tk)
    lax.fori_loop(0, first_diag, body, 0)
    step(first_diag, 0, tq, True)
    step(first_diag + 1, tk, tq - tk, True)
    outs = []
    for hd in range(H):
        acc = acc_ref[hd]
        outs.append(acc[:dv] * (1.0 / acc[dv:dv + 1]))
    o_ref[...] = jnp.concatenate(outs, axis=0).T.astype(BF16)


def _attention(q, k, vt, dv, name):
    B, H, S, dq = q.shape
    _, _, _, dva, tk = vt.shape
    tq = 2 * tk
    return pl.pallas_call(
        functools.partial(_attn_kernel, dv=dv),
        grid=(B, S // tq),
        in_specs=[pl.BlockSpec((None, H, tq, dq), lambda b, i: (b, 0, i, 0)),
                  pl.BlockSpec((None, H, S, dq), lambda b, i: (b, 0, 0, 0)),
                  pl.BlockSpec((None, H, S // tk, dva, tk), lambda b, i: (b, 0, 0, 0, 0))],
        out_specs=pl.BlockSpec((None, tq, H * dv), lambda b, i: (b, i, 0)),
        out_shape=jax.ShapeDtypeStruct((B, S, H * dv), BF16),
        scratch_shapes=[pltpu.VMEM((H, dva, tq), F32), pltpu.VMEM((H, 8, tq), F32),
                        pltpu.VMEM((H, tk, tq), F32)],
        compiler_params=_params(("parallel", "arbitrary")),
        name=name,
    )(q, k, vt)


def _merge_kernel(x_ref, mod_ref, gpre_ref, gpost_ref, wg_ref, ya_ref, yb_ref, yc_ref,
                  wa_ref, wb_ref, wc_ref, wo_ref, o_ref):
    D = D_MODEL
    x = x_ref[...]
    h = (_rms(x, gpre_ref[...]) * (1.0 + mod_ref[1:2, :]) + mod_ref[0:1, :]).astype(BF16)
    merged = None
    for j, (y_ref, w_ref) in enumerate(((ya_ref, wa_ref), (yb_ref, wb_ref), (yc_ref, wc_ref))):
        gate = _sigmoid(jnp.dot(h, wg_ref[:, j * D:(j + 1) * D], preferred_element_type=F32))
        term = gate * jnp.dot(y_ref[...], w_ref[...], preferred_element_type=F32)
        merged = term if merged is None else merged + term
    out = _bdot(merged, wo_ref[...])
    o_ref[...] = x + mod_ref[2:3, :] * _rms(out, gpost_ref[...])


def _merge(l, x, mod, gpre, gpost, wg, ya, yb, yc, wa, wb, wc, wo, tm):
    B, S, D = x.shape
    tile = lambda w: pl.BlockSpec((None, tm, w), lambda b, i: (b, i, 0))
    return pl.pallas_call(
        _merge_kernel,
        grid=(B, S // tm),
        in_specs=[tile(D), pl.BlockSpec((None, None, 6, D), lambda b, i: (l, b, 0, 0)), _layer(gpre, l),
                  _layer(gpost, l), _layer(wg, l), tile(ya.shape[-1]), tile(yb.shape[-1]),
                  tile(yc.shape[-1]), _layer(wa, l), _layer(wb, l), _layer(wc, l),
                  _layer(wo, l)],
        out_specs=tile(D),
        out_shape=jax.ShapeDtypeStruct((B, S, D), F32),
        compiler_params=_params(("parallel", "parallel")),
        name="merge_out",
    )(x, mod, gpre, gpost, wg, ya, yb, yc, wa, wb, wc, wo)


def _ffn_kernel(x_ref, mod_ref, gpre_ref, gpost_ref, wu_ref, wd_ref, o_ref):
    D = D_MODEL
    x = x_ref[...]
    h = (_rms(x, gpre_ref[...]) * (1.0 + mod_ref[4:5, :]) + mod_ref[3:4, :]).astype(BF16)
    acc = None
    for j in range(D_FF // D):
        u = jnp.maximum(jnp.dot(h, wu_ref[:, j * D:(j + 1) * D], preferred_element_type=F32), 0.0)
        term = jnp.dot((u * u).astype(BF16), wd_ref[j * D:(j + 1) * D, :], preferred_element_type=F32)
        acc = term if acc is None else acc + term
    o_ref[...] = x + mod_ref[5:6, :] * _rms(acc, gpost_ref[...])


def _ffn(l, x, mod, gpre, gpost, wu, wd, tm):
    B, S, D = x.shape
    tile = pl.BlockSpec((None, tm, D), lambda b, i: (b, i, 0))
    return pl.pallas_call(
        _ffn_kernel,
        grid=(B, S // tm),
        in_specs=[tile, pl.BlockSpec((None, None, 6, D), lambda b, i: (l, b, 0, 0)), _layer(gpre, l),
                  _layer(gpost, l), _layer(wu, l), _layer(wd, l)],
        out_specs=tile,
        out_shape=jax.ShapeDtypeStruct((B, S, D), F32),
        compiler_params=_params(("parallel", "parallel")),
        name="ffn",
    )(x, mod, gpre, gpost, wu, wd)


def _pad_heads(w, dh, to):
    L, K, _ = w.shape
    return jnp.pad(w.reshape(L, K, N_HEADS, dh),
                   ((0, 0), (0, 0), (0, 0), (0, to - dh))).reshape(L, K, N_HEADS * to)


def _fox_selector():
    row = jnp.arange(384)[:, None]
    col = jnp.arange(512)[None, :]
    part, hd = row // 128, row % 128
    valid = hd < N_HEADS
    sel = (jnp.where(valid & (col == hd * 128 + 64 + part), 1.0, 0.0)
           - jnp.where(valid & (col == hd * 128 + 67 + part), 1.0, 0.0))
    return sel.astype(BF16)


def _rows(v):
    return v.reshape(v.shape[0], 1, -1)


def kernel(x, c, positions, w_in, mu_shift, w0, w_decay_up, a0, w_aaa_up, w_gate_up, k_k, k_a, r_k,
           ln_x_g, ln_x_b, b_forget, q_norm_g, w_q_up, kv_norm_g, w_kv_up, w_branch_a, w_branch_b,
           w_branch_c, w_out, w_mod, b_mod, norm_mix_pre, norm_mix_post, norm_ffn_pre, norm_ffn_post,
           w_ffn_up, w_ffn_down):
    B, S, D = x.shape
    L = w_in.shape[0]
    tm = min(512, S)
    bf16 = lambda w: w.astype(BF16)

    c8 = jnp.pad(c, ((0, 8 - B), (0, 0)))
    mod = _modulation(c8, w_mod, b_mod)[:, :B].reshape(L, B, 6, D)
    cos_t, sin_t = _rope_tables(positions, tm)
    sel = _fox_selector()

    fq_scale = HEAD64 ** -0.5 * LOG2E
    mq_scale = (QK_NOPE_DIM + QK_ROPE_DIM) ** -0.5 * LOG2E
    w1 = jnp.concatenate([
        bf16(w_in[:, :, :_FOX0]),
        bf16(w_in[:, :, _FOX0:_FOX0 + 256] * fq_scale),
        bf16(w_in[:, :, _FOX0 + 256:_FOX0 + 768]),
        jnp.pad(bf16(w_in[:, :, _FOX0 + 768:_MLA0]), ((0, 0), (0, 0), (0, 128 - N_HEADS))),
        bf16(w_in[:, :, _MLA0:_MLA0 + Q_LORA_RANK + KV_LORA_RANK]),
        jnp.pad(bf16(w_in[:, :, _GATE0 - QK_ROPE_DIM:_GATE0]), ((0, 0), (0, 0), (0, 128 - QK_ROPE_DIM))),
    ], axis=-1)
    w_gates = bf16(w_in[:, :, _GATE0:])
    bf = jnp.pad(_rows(b_forget), ((0, 0), (0, 0), (0, 128 - N_HEADS)))
    wq = _pad_heads(bf16(w_q_up * mq_scale), QK_NOPE_DIM + QK_ROPE_DIM, 256)
    wkv = bf16(w_kv_up)
    zeros64 = jnp.zeros((L, 64, RWKV_WIDTH), F32)
    wd = jnp.concatenate([w_decay_up, zeros64], axis=1)
    wa = jnp.concatenate([zeros64, w_aaa_up], axis=1)
    wba, wbb, wbc, wo = bf16(w_branch_a), bf16(w_branch_b), bf16(w_branch_c), bf16(w_out)
    wu, wdn = bf16(w_ffn_up), bf16(w_ffn_down)
    g_mix_pre, g_mix_post = _rows(norm_mix_pre), _rows(norm_mix_post)
    g_ffn_pre, g_ffn_post = _rows(norm_ffn_pre), _rows(norm_ffn_post)
    rwkv_rows = [_rows(v) for v in (mu_shift, w0)] + [wd, _rows(a0), wa, w_gate_up] + [
        _rows(v) for v in (k_k, k_a, r_k.reshape(L, -1), ln_x_g, ln_x_b)]
    qg, kvg = _rows(q_norm_g), _rows(kv_norm_g)

    for l in range(L):
        za, fq, fk, fv, mq, mk, mv = _inproj(l, x, mod, g_mix_pre, w1, bf, sel, qg, wq,
                                             kvg, wkv, cos_t, sin_t, tm)
        ya = _rwkv(l, za, *rwkv_rows, tm)
        yb = _attention(fq, fk, fv, HEAD64, "fox_attention")
        yc = _attention(mq, mk, mv, V_HEAD_DIM, "mla_attention")
        x = _merge(l, x, mod, g_mix_pre, g_mix_post, w_gates, ya, yb, yc, wba, wbb, wbc, wo, tm)
        x = _ffn(l, x, mod, g_ffn_pre, g_ffn_post, wu, wdn, tm)
    return x
```

```python
import functools

import jax
import jax.numpy as jnp
from jax import lax
from jax.experimental import pallas as pl
from jax.experimental.pallas import tpu as pltpu

F32 = jnp.float32
BF16 = jnp.bfloat16

D_MODEL = 1024
N_HEADS = 4
HEAD64 = 64
RWKV_WIDTH = N_HEADS * HEAD64
QK_NOPE_DIM = 128
QK_ROPE_DIM = 64
V_HEAD_DIM = 128
Q_LORA_RANK = 384
KV_LORA_RANK = 256
ROPE_THETA = 10000.0
D_FF = 4 * D_MODEL
NORM_EPS = 1e-6
RWKV_LN_EPS = 64e-5
MASK_VALUE = -1e30
LOG2E = 1.4426950408889634
EXP_NEG_HALF = 0.6065306597126334
CHUNK = 64

_FOX0 = 1024
_MLA0 = _FOX0 + 3 * 256 + N_HEADS
_GATE0 = _MLA0 + Q_LORA_RANK + KV_LORA_RANK + QK_ROPE_DIM

_C_RWKV = 0
_C_FQ = 1024
_C_FK = 1280
_C_FV = 1536
_C_FF = 1792
_C_QL = 1920
_C_KVL = 2304
_C_KPE = 2560
_C_END = 2688

VMEM_LIMIT = 56 * 1024 * 1024


def _bdot(a, b):
    return jnp.dot(a.astype(BF16), b.astype(BF16), preferred_element_type=F32)


def _bdot_nt(a, b):
    return lax.dot_general(a.astype(BF16), b.astype(BF16), (((1,), (1,)), ((), ())),
                           preferred_element_type=F32)


def _bdot_tn(a, b):
    return lax.dot_general(a.astype(BF16), b.astype(BF16), (((0,), (0,)), ((), ())),
                           preferred_element_type=F32)


def _split2_dot(a, b_bf16):
    hi = a.astype(BF16)
    lo = (a - hi.astype(F32)).astype(BF16)
    return (jnp.dot(hi, b_bf16, preferred_element_type=F32)
            + jnp.dot(lo, b_bf16, preferred_element_type=F32))


def _split3(a):
    hi = a.astype(BF16)
    r1 = a - hi.astype(F32)
    mid = r1.astype(BF16)
    lo = (r1 - mid.astype(F32)).astype(BF16)
    return hi, mid, lo


def _rms(x, gain):
    return x * lax.rsqrt(jnp.mean(x * x, axis=-1, keepdims=True) + NORM_EPS) * gain


def _softplus(y):
    return jnp.maximum(y, 0.0) + jnp.log(1.0 + jnp.exp(-jnp.abs(y)))


def _sigmoid(y):
    return 1.0 / (1.0 + jnp.exp(-y))


def _iota(shape, dim):
    return lax.broadcasted_iota(jnp.int32, shape, dim)


def _full(shape):
    return pl.BlockSpec(shape, lambda *_: (0,) * len(shape))


def _layer(arr, l):
    rest = arr.shape[1:]
    return pl.BlockSpec((None,) + rest, lambda *_: (l,) + (0,) * len(rest))


def _params(sem):
    return pltpu.CompilerParams(dimension_semantics=sem, vmem_limit_bytes=VMEM_LIMIT)


def _mod_kernel(c_ref, w_ref, b_ref, o_ref):
    c = c_ref[...]
    o_ref[...] = _split2_dot(c * _sigmoid(c), w_ref[...].astype(BF16)) + b_ref[...]


def _modulation(c8, w_mod, b_mod):
    L, D, D6 = w_mod.shape
    return pl.pallas_call(
        _mod_kernel,
        grid=(L, D6 // D),
        in_specs=[pl.BlockSpec((8, D), lambda l, j: (0, 0)),
                  pl.BlockSpec((None, D, D), lambda l, j: (l, 0, j)),
                  pl.BlockSpec((None, 1, D), lambda l, j: (l, 0, j))],
        out_specs=pl.BlockSpec((None, 8, D), lambda l, j: (l, 0, j)),
        out_shape=jax.ShapeDtypeStruct((L, 8, D6), F32),
        compiler_params=_params(("parallel", "parallel")),
        name="adaln_mod",
    )(c8, w_mod, b_mod.reshape(L, 1, D6))


def _rope_kernel(pos_ref, inv_ref, cos_ref, sin_ref):
    ang = pos_ref[...] * inv_ref[...]
    lane = _iota(ang.shape, 1)
    cos_ref[...] = jnp.where(lane < 64, jnp.cos(ang), 0.0)
    s = jnp.sin(ang)
    sin_ref[...] = jnp.where(lane < 32, -s, jnp.where(lane < 64, s, 0.0))


def _rope_tables(positions, tm):
    B, S = positions.shape
    inv_freq = ROPE_THETA ** (-jnp.arange(0, QK_ROPE_DIM, 2, dtype=F32) / QK_ROPE_DIM)
    inv128 = jnp.tile(inv_freq, 4).reshape(1, 128)
    pos = positions.astype(F32).reshape(B, S, 1)
    out = jax.ShapeDtypeStruct((B, S, 128), F32)
    return pl.pallas_call(
        _rope_kernel,
        grid=(B, S // tm),
        in_specs=[pl.BlockSpec((None, tm, 1), lambda b, i: (b, i, 0)), _full((1, 128))],
        out_specs=[pl.BlockSpec((None, tm, 128), lambda b, i: (b, i, 0))] * 2,
        out_shape=[out, out],
        compiler_params=_params(("parallel", "parallel")),
        name="rope_tables",
    )(pos, inv128)


def _rope128(x, cos_t, sin_t):
    lane = _iota(x.shape, 1)
    partner = jnp.where(lane < 32, pltpu.roll(x, 96, 1), pltpu.roll(x, 32, 1))
    return x * cos_t + partner * sin_t


def _inproj_kernel(x_ref, mod_ref, gpre_ref, w1_ref, bf_ref, sel_ref,
                   qg_ref, wq_ref, kvg_ref, wkv_ref, cos_ref, sin_ref,
                   za_ref, fq_ref, fk_ref, fv_ref, mq_ref, mk_ref, mv_ref, carry_ref):
    i = pl.program_id(1)
    tm = x_ref.shape[0]

    @pl.when(i == 0)
    def _():
        carry_ref[...] = jnp.zeros_like(carry_ref)

    x = x_ref[...]
    h = (_rms(x, gpre_ref[...]) * (1.0 + mod_ref[1:2, :]) + mod_ref[0:1, :]).astype(BF16)

    za_ref[...] = jnp.dot(h, w1_ref[:, _C_RWKV:_C_FQ], preferred_element_type=F32)

    f = jnp.dot(h, w1_ref[:, _C_FF:_C_QL], preferred_element_type=F32) + bf_ref[...]
    logf = -_softplus(-f)
    tri = (_iota((tm, tm), 0) >= _iota((tm, tm), 1)).astype(BF16)
    l_hi, l_mid, l_lo = _split3(logf)
    cum = (jnp.dot(tri, l_hi, preferred_element_type=F32)
           + jnp.dot(tri, l_mid, preferred_element_type=F32)
           + jnp.dot(tri, l_lo, preferred_element_type=F32)) + carry_ref[0:1, :]
    carry_ref[0:1, :] = cum[tm - 1:tm, :]
    cc = jnp.concatenate(_split3(cum * LOG2E), axis=1)
    aug = jnp.dot(cc, sel_ref[...], preferred_element_type=F32)
    fqk = jnp.dot(h, w1_ref[:, _C_FQ:_C_FV], preferred_element_type=F32)
    lane = _iota((1, 128), 1)
    q_ones = (lane >= 67) & (lane < 70)
    k_ones = (lane >= 64) & (lane < 67)
    fv_t = jnp.dot(h, w1_ref[:, _C_FV:_C_FF], preferred_element_type=F32).T
    ones_rows = (_iota((16, tm), 0) == 0).astype(BF16)
    for hd in range(N_HEADS):
        aug_h = aug[:, 128 * hd:128 * hd + 128]
        q_grp = fqk[:, 128 * (hd // 2):128 * (hd // 2) + 128]
        k_grp = fqk[:, 256 + 128 * (hd // 2):256 + 128 * (hd // 2) + 128]
        if hd % 2:
            q_grp = pltpu.roll(q_grp, 64, 1)
            k_grp = pltpu.roll(k_grp, 64, 1)
        fq_ref[hd] = jnp.where(lane < 64, q_grp, jnp.where(q_ones, 1.0, aug_h)).astype(BF16)
        fk_ref[hd] = jnp.where(lane < 64, k_grp, jnp.where(k_ones, 1.0, aug_h)).astype(BF16)
        fv_ref[hd] = jnp.concatenate([fv_t[64 * hd:64 * hd + 64].astype(BF16), ones_rows], axis=0)

    cos_t = cos_ref[...]
    sin_t = sin_ref[...]
    q_lat = jnp.dot(h, w1_ref[:, _C_QL:_C_KVL], preferred_element_type=F32)
    q = _bdot(_rms(q_lat, qg_ref[...]), wq_ref[...])
    kv_lat = jnp.dot(h, w1_ref[:, _C_KVL:_C_KPE], preferred_element_type=F32)
    kv = _bdot(_rms(kv_lat, kvg_ref[...]), wkv_ref[...])
    k_pe = _rope128(jnp.dot(h, w1_ref[:, _C_KPE:_C_END], preferred_element_type=F32), cos_t, sin_t)
    k_pe = k_pe.astype(BF16)
    for hd in range(N_HEADS):
        o = 256 * hd
        q_pe = _rope128(q[:, o + 128:o + 256], cos_t, sin_t)
        mq_ref[hd] = jnp.concatenate([q[:, o:o + 128], q_pe], axis=1).astype(BF16)
        mk_ref[hd] = jnp.concatenate([kv[:, o:o + 128].astype(BF16), k_pe], axis=1)
        mv_ref[hd] = jnp.concatenate([kv[:, o + 128:o + 256].T.astype(BF16), ones_rows], axis=0)


def _inproj(l, x, mod, gpre, w1, bf, sel, qg, wq, kvg, wkv, cos_t, sin_t, tm):
    B, S, D = x.shape
    H = N_HEADS
    tile = lambda w: pl.BlockSpec((None, tm, w), lambda b, i: (b, i, 0))
    head = lambda w: pl.BlockSpec((None, H, tm, w), lambda b, i: (b, 0, i, 0))
    hs = lambda w: jax.ShapeDtypeStruct((B, H, S, w), BF16)
    head_t = lambda r: pl.BlockSpec((None, H, None, r, tm), lambda b, i: (b, 0, i, 0, 0))
    hs_t = lambda r: jax.ShapeDtypeStruct((B, H, S // tm, r, tm), BF16)
    return pl.pallas_call(
        _inproj_kernel,
        grid=(B, S // tm),
        in_specs=[tile(D), pl.BlockSpec((None, None, 6, D), lambda b, i: (l, b, 0, 0)), _layer(gpre, l),
                  _layer(w1, l), _layer(bf, l), _full(sel.shape), _layer(qg, l), _layer(wq, l),
                  _layer(kvg, l), _layer(wkv, l), tile(128), tile(128)],
        out_specs=[tile(1024), head(128), head(128), head_t(80), head(256), head(256), head_t(144)],
        out_shape=[jax.ShapeDtypeStruct((B, S, 1024), F32), hs(128), hs(128), hs_t(80),
                   hs(256), hs(256), hs_t(144)],
        scratch_shapes=[pltpu.VMEM((8, 128), F32)],
        compiler_params=_params(("parallel", "arbitrary")),
        name="in_proj",
    )(x, mod, gpre, w1, bf, sel, qg, wq, kvg, wkv, cos_t, sin_t)


def _rwkv_kernel(z_ref, mu_ref, w0_ref, wd_ref, a0_ref, wa_ref, wg_ref, kk_ref, ka_ref, rk_ref,
                 lng_ref, lnb_ref, y_ref,
                 prev_ref, h_ref, rt_ref, at_ref, bt_ref, kt_ref, bh_ref, kh_ref, v_ref, gam_ref,
                 ys_ref, yp_ref, gw_ref, gw2_ref, pw_ref, rp_ref):
    i = pl.program_id(1)
    tm = z_ref.shape[0]
    W = RWKV_WIDTH
    C = CHUNK

    @pl.when(i == 0)
    def _():
        prev_ref[...] = jnp.zeros_like(prev_ref)
        h_ref[...] = jnp.zeros_like(h_ref)

    z = z_ref[...]
    zroll = pltpu.roll(z, 1, 0)
    zprev = jnp.concatenate([jnp.where(_iota((8, 1), 0) == 0, prev_ref[0:1, :], zroll[0:8]), zroll[8:]], axis=0)
    prev_ref[0:1, :] = z[tm - 1:tm, :]
    zm = z + (zprev - z) * mu_ref[...]
    r = zm[:, 0:W]
    k = zm[:, W:2 * W]
    v = zm[:, 2 * W:3 * W]
    lora_in = zm[:, 3 * W:3 * W + 128]
    gd = zm[:, 3 * W + 128:3 * W + 256]
    lw = -EXP_NEG_HALF * _sigmoid(w0_ref[...] + _bdot(jnp.tanh(lora_in), wd_ref[...]))
    a = _sigmoid(a0_ref[...] + _bdot(lora_in, wa_ref[...]))
    g = _bdot(_sigmoid(gd), wg_ref[...])
    seg = ((_iota((W, W), 0) // HEAD64) == (_iota((W, W), 1) // HEAD64)).astype(BF16)
    kk = k * kk_ref[...]
    kk = kk * lax.rsqrt(jnp.maximum(_bdot(kk * kk, seg), 1e-24))
    k2 = k * (1.0 + (a - 1.0) * ka_ref[...])

    tri = (_iota((C, C), 0) >= _iota((C, C), 1)).astype(BF16)
    lw3 = _split3(lw)
    cls, ces = [], []
    for c in range(tm // C):
        cl_c = sum(jnp.dot(tri, part[c * C:(c + 1) * C, :], preferred_element_type=F32) for part in lw3)
        cls.append(cl_c)
        ces.append(jnp.broadcast_to(cl_c[C - 1:C, :], (C, W)))
    cl = jnp.concatenate(cls, axis=0)
    ce = jnp.concatenate(ces, axis=0)
    e_neg = jnp.exp(-cl)
    e_end = jnp.exp(ce - cl)
    rt_ref[...] = r * jnp.exp(cl)
    at_ref[...] = -kk * jnp.exp(cl - lw)
    bt_ref[...] = kk * a * e_neg
    kt_ref[...] = k2 * e_neg
    bh_ref[...] = kk * a * e_end
    kh_ref[...] = k2 * e_end
    v_ref[...] = v
    gam_ref[...] = jnp.exp(ce)

    block_mask = (_iota((W, W), 0) // C) == (_iota((W, W), 1) // C)
    t_row = _iota((C, W), 0)
    s_col = _iota((C, W), 1) % C
    strict_lower = t_row > s_col
    incl_lower = t_row >= s_col
    eye_w = (t_row == s_col).astype(F32)

    def block(x):
        return jnp.where(block_mask, jnp.concatenate([x] * N_HEADS, axis=0), 0.0).astype(BF16)

    def fold(x_bd):
        return x_bd[0:C] + x_bd[C:2 * C] + x_bd[2 * C:3 * C] + x_bd[3 * C:4 * C]

    group = 8

    def chunk_group(gi, carry):
        rows = [pl.ds(pl.multiple_of((gi * group + c) * C, C), C) for c in range(group)]
        cs = range(group)
        a_n = [at_ref[rows[c], :] for c in cs]
        a_s = [block(a_n[c]) for c in cs]
        m = [_bdot_nt(jnp.concatenate([a_n[c], rt_ref[rows[c], :]], axis=0),
                      jnp.concatenate([block(bt_ref[rows[c], :]), block(kt_ref[rows[c], :])], axis=0))
             for c in cs]
        a_ab = [jnp.where(strict_lower, m[c][:C, :W], 0.0) for c in cs]
        m_rb = [jnp.where(incl_lower, m[c][C:, :W], 0.0) for c in cs]
        av = [_bdot(jnp.concatenate([jnp.where(strict_lower, m[c][:C, W:], 0.0),
                                     jnp.where(incl_lower, m[c][C:, W:], 0.0)], axis=0),
                    block(v_ref[rows[c], :])) for c in cs]
        t_inv = [eye_w + a_ab[c] for c in cs]
        p = a_ab
        for _ in range(5):
            p = [_bdot(p[c], block(p[c])) for c in cs]
            t_inv = [t_inv[c] + _bdot(t_inv[c], block(p[c])) for c in cs]
        x = [_bdot(t_inv[c], jnp.concatenate([a_s[c], block(av[c][:C])], axis=1)) for c in cs]
        ry = [_bdot(m_rb[c], jnp.concatenate([block(x[c][:, :W]), block(x[c][:, W:])], axis=1)) for c in cs]
        for c in cs:
            rp_ref[rows[c], :] = (rt_ref[rows[c], :] + ry[c][:, :W]).astype(BF16)
            yp_ref[rows[c], :] = av[c][C:] + ry[c][:, W:]
            bh_n = bh_ref[rows[c], :]
            p_bd = jnp.where(block_mask, _bdot_tn(bh_n, x[c][:, :W]), 0.0)
            g_bd = jnp.where(block_mask,
                             _bdot_tn(jnp.concatenate([bh_n, kh_ref[rows[c], :]], axis=0),
                                      jnp.concatenate([x[c][:, W:], v_ref[rows[c], :]], axis=0)), 0.0)
            pw_ref[rows[c], :] = fold(p_bd)
            gw_ref[rows[c], :] = fold(g_bd)
            gam_row = gam_ref[pl.ds(pl.multiple_of((gi * group + c) * C, C), 1), :]
            gw2_ref[rows[c], :] = _split2_dot(eye_w * gam_row, seg)
        return carry

    lax.fori_loop(0, tm // (C * group), chunk_group, 0)

    n_chunks = tm // C

    def chunk_map(c):
        rows = slice(c * C, (c + 1) * C)
        return gw2_ref[rows, :], pw_ref[rows, :], gw_ref[rows, :], gam_ref[c * C:c * C + 1, :]

    def compose(first, second):
        d1, p1, g1, row1 = first
        d2, p2, g2, row2 = second
        cross = _bdot(p2, jnp.concatenate([block(p1), block(g1)], axis=1))
        return (d2 * d1, d2 * p1 + p2 * row1 + cross[:, :W], d2 * g1 + cross[:, W:] + g2, row2 * row1)

    def apply(mp, h, ph):
        return mp[0] * h + ph + mp[2]

    singles = [chunk_map(c) for c in range(n_chunks)]
    pairs = [compose(singles[2 * k], singles[2 * k + 1]) for k in range(n_chunks // 2)]
    quads = [compose(pairs[2 * k], pairs[2 * k + 1]) for k in range(n_chunks // 4)]
    starts = {0: h_ref[...]}
    for c in range(n_chunks):
        rows = slice(c * C, (c + 1) * C)
        spans = [(1, singles[c])]
        if c % 2 == 0:
            spans.append((2, pairs[c // 2]))
        if c % 4 == 0:
            spans.append((4, quads[c // 4]))
        spans = [(n, mp) for n, mp in spans if c + n not in starts]
        lhs = jnp.concatenate([rp_ref[rows, :]] + [mp[1].astype(BF16) for _, mp in spans], axis=0)
        out = _bdot(lhs, block(starts[c]))
        ys_ref[rows, :] = out[:C] + yp_ref[rows, :]
        for k, (n, mp) in enumerate(spans):
            starts[c + n] = apply(mp, starts[c], out[(k + 1) * C:(k + 2) * C])
    h_ref[...] = starts[n_chunks]

    y = ys_ref[...]
    inv = 1.0 / HEAD64
    mean = _bdot(y, seg) * inv
    d = y - mean
    var = _bdot(d * d, seg) * inv
    yn = d * lax.rsqrt(var + RWKV_LN_EPS) * lng_ref[...] + lnb_ref[...]
    bonus = _bdot(r * k2 * rk_ref[...], seg) * v
    y_ref[...] = ((yn + bonus) * g).astype(BF16)


def _rwkv(l, za, mu, w0, wd, a0, wa, wg, k_k, k_a, r_k, ln_g, ln_b, tm):
    B, S, _ = za.shape
    W = RWKV_WIDTH
    smalls = [mu, w0, wd, a0, wa, wg, k_k, k_a, r_k, ln_g, ln_b]
    return pl.pallas_call(
        _rwkv_kernel,
        grid=(B, S // tm),
        in_specs=[pl.BlockSpec((None, tm, 1024), lambda b, i: (b, i, 0))] + [_layer(s, l) for s in smalls],
        out_specs=pl.BlockSpec((None, tm, W), lambda b, i: (b, i, 0)),
        out_shape=jax.ShapeDtypeStruct((B, S, W), BF16),
        scratch_shapes=[pltpu.VMEM((8, 1024), F32), pltpu.VMEM((CHUNK, W), F32)]
                       + [pltpu.VMEM((tm, W), F32)] * 13 + [pltpu.VMEM((tm, W), BF16)],
        compiler_params=_params(("parallel", "arbitrary")),
        name="rwkv7_chunked",
    )(za, *smalls)


def _attn_kernel(q_ref, k_ref, vt_ref, o_ref, acc_ref, m_ref, s_ref, *, dv):
    qi = pl.program_id(1)
    H, tq, _ = q_ref.shape
    tk = vt_ref.shape[-1]
    acc_ref[...] = jnp.zeros_like(acc_ref)
    m_ref[...] = jnp.full_like(m_ref, MASK_VALUE)

    def step(j, nblk, q0, qn, masked):
        nk = nblk * tk
        rows = pl.ds(pl.multiple_of(j * tk, tk), nk)
        lanes = slice(q0, q0 + qn)
        if masked:
            key_le_query = _iota((nk, qn), 0) <= _iota((nk, qn), 1)
        m_news, alphas = [], []
        for hd in range(H):
            st = _bdot_nt(k_ref[hd, rows, :], q_ref[hd, lanes, :])
            if masked:
                st = jnp.where(key_le_query, st, MASK_VALUE)
            s_ref[hd, 0:nk, lanes] = st
            m_old = m_ref[hd, 0:1, lanes]
            m_new = jnp.maximum(m_old, jnp.max(st, axis=0, keepdims=True))
            m_ref[hd, 0:1, lanes] = m_new
            m_news.append(m_new)
            alphas.append(jnp.exp2(m_old - m_new))
        for hd in range(H):
            p = jnp.exp2(s_ref[hd, 0:nk, lanes] - m_news[hd]).astype(BF16)
            pv = sum(jnp.dot(vt_ref[hd, j + b], p[b * tk:(b + 1) * tk], preferred_element_type=F32)
                     for b in range(nblk))
            acc_ref[hd, :, lanes] = alphas[hd] * acc_ref[hd, :, lanes] + pv

    blocks_per_tile = tq // tk

    def body(jj, carry):
        step(jj * blocks_per_tile, blocks_per_tile, 0, tq, False)
        return carry

    lax.fori_loop(0, qi, body, 0)
    first_diag = qi * blocks_per_tile
    step(first_diag, 1, 0, tq, True)
    step(first_diag + 1, 1, tk, tq - tk, True)
    outs = []
    for hd in range(H):
        acc = acc_ref[hd]
        outs.append(acc[:dv] * (1.0 / acc[dv:dv + 1]))
    o_ref[...] = jnp.concatenate(outs, axis=0).T.astype(BF16)


def _attention(q, k, vt, dv, name):
    B, H, S, dq = q.shape
    _, _, _, dva, tk = vt.shape
    tq = 2 * tk
    return pl.pallas_call(
        functools.partial(_attn_kernel, dv=dv),
        grid=(B, S // tq),
        in_specs=[pl.BlockSpec((None, H, tq, dq), lambda b, i: (b, 0, i, 0)),
                  pl.BlockSpec((None, H, S, dq), lambda b, i: (b, 0, 0, 0)),
                  pl.BlockSpec((None, H, S // tk, dva, tk), lambda b, i: (b, 0, 0, 0, 0))],
        out_specs=pl.BlockSpec((None, tq, H * dv), lambda b, i: (b, i, 0)),
        out_shape=jax.ShapeDtypeStruct((B, S, H * dv), BF16),
        scratch_shapes=[pltpu.VMEM((H, dva, tq), F32), pltpu.VMEM((H, 8, tq), F32),
                        pltpu.VMEM((H, tq, tq), F32)],
        compiler_params=_params(("parallel", "arbitrary")),
        name=name,
    )(q, k, vt)


def _merge_kernel(x_ref, mod_ref, gpre_ref, gpost_ref, wg_ref, ya_ref, yb_ref, yc_ref,
                  wa_ref, wb_ref, wc_ref, wo_ref, o_ref):
    D = D_MODEL
    x = x_ref[...]
    h = (_rms(x, gpre_ref[...]) * (1.0 + mod_ref[1:2, :]) + mod_ref[0:1, :]).astype(BF16)
    merged = None
    for j, (y_ref, w_ref) in enumerate(((ya_ref, wa_ref), (yb_ref, wb_ref), (yc_ref, wc_ref))):
        gate = _sigmoid(jnp.dot(h, wg_ref[:, j * D:(j + 1) * D], preferred_element_type=F32))
        term = gate * jnp.dot(y_ref[...], w_ref[...], preferred_element_type=F32)
        merged = term if merged is None else merged + term
    out = _bdot(merged, wo_ref[...])
    o_ref[...] = x + mod_ref[2:3, :] * _rms(out, gpost_ref[...])


def _merge(l, x, mod, gpre, gpost, wg, ya, yb, yc, wa, wb, wc, wo, tm):
    B, S, D = x.shape
    tile = lambda w: pl.BlockSpec((None, tm, w), lambda b, i: (b, i, 0))
    return pl.pallas_call(
        _merge_kernel,
        grid=(B, S // tm),
        in_specs=[tile(D), pl.BlockSpec((None, None, 6, D), lambda b, i: (l, b, 0, 0)), _layer(gpre, l),
                  _layer(gpost, l), _layer(wg, l), tile(ya.shape[-1]), tile(yb.shape[-1]),
                  tile(yc.shape[-1]), _layer(wa, l), _layer(wb, l), _layer(wc, l),
                  _layer(wo, l)],
        out_specs=tile(D),
        out_shape=jax.ShapeDtypeStruct((B, S, D), F32),
        compiler_params=_params(("parallel", "parallel")),
        name="merge_out",
    )(x, mod, gpre, gpost, wg, ya, yb, yc, wa, wb, wc, wo)


def _ffn_kernel(x_ref, mod_ref, gpre_ref, gpost_ref, wu_ref, wd_ref, o_ref):
    D = D_MODEL
    x = x_ref[...]
    h = (_rms(x, gpre_ref[...]) * (1.0 + mod_ref[4:5, :]) + mod_ref[3:4, :]).astype(BF16)
    acc = None
    for j in range(D_FF // D):
        u = jnp.maximum(jnp.dot(h, wu_ref[:, j * D:(j + 1) * D], preferred_element_type=F32), 0.0)
        term = jnp.dot((u * u).astype(BF16), wd_ref[j * D:(j + 1) * D, :], preferred_element_type=F32)
        acc = term if acc is None else acc + term
    o_ref[...] = x + mod_ref[5:6, :] * _rms(acc, gpost_ref[...])


def _ffn(l, x, mod, gpre, gpost, wu, wd, tm):
    B, S, D = x.shape
    tile = pl.BlockSpec((None, tm, D), lambda b, i: (b, i, 0))
    return pl.pallas_call(
        _ffn_kernel,
        grid=(B, S // tm),
        in_specs=[tile, pl.BlockSpec((None, None, 6, D), lambda b, i: (l, b, 0, 0)), _layer(gpre, l),
                  _layer(gpost, l), _layer(wu, l), _layer(wd, l)],
        out_specs=tile,
        out_shape=jax.ShapeDtypeStruct((B, S, D), F32),
        compiler_params=_params(("parallel", "parallel")),
        name="ffn",
    )(x, mod, gpre, gpost, wu, wd)


def _inproj_weights_kernel(w_ref, w1_ref, wg_ref, *, fq_scale):
    w = w_ref[...]
    rb = w.shape[0]
    zeros = lambda n: jnp.zeros((rb, n), F32)
    w1 = jnp.concatenate([
        w[:, :_FOX0],
        w[:, _FOX0:_FOX0 + 256] * fq_scale,
        w[:, _FOX0 + 256:_FOX0 + 768],
        w[:, _FOX0 + 768:_MLA0], zeros(128 - N_HEADS),
        w[:, _MLA0:_MLA0 + Q_LORA_RANK + KV_LORA_RANK],
        w[:, _GATE0 - QK_ROPE_DIM:_GATE0], zeros(128 - QK_ROPE_DIM),
    ], axis=1)
    w1_ref[...] = w1.astype(BF16)
    wg_ref[...] = w[:, _GATE0:].astype(BF16)


def _inproj_weights(w_in, fq_scale, rb=256):
    L, D, n_in = w_in.shape
    n_gate = n_in - _GATE0
    return pl.pallas_call(
        functools.partial(_inproj_weights_kernel, fq_scale=fq_scale),
        grid=(L, D // rb),
        in_specs=[pl.BlockSpec((None, rb, n_in), lambda l, i: (l, i, 0))],
        out_specs=[pl.BlockSpec((None, rb, _C_END), lambda l, i: (l, i, 0)),
                   pl.BlockSpec((None, rb, n_gate), lambda l, i: (l, i, 0))],
        out_shape=[jax.ShapeDtypeStruct((L, D, _C_END), BF16),
                   jax.ShapeDtypeStruct((L, D, n_gate), BF16)],
        compiler_params=_params(("parallel", "parallel")),
        name="inproj_weights",
    )(w_in)


def _pad_heads(w, dh, to):
    L, K, _ = w.shape
    return jnp.pad(w.reshape(L, K, N_HEADS, dh),
                   ((0, 0), (0, 0), (0, 0), (0, to - dh))).reshape(L, K, N_HEADS * to)


def _fox_selector():
    row = jnp.arange(384)[:, None]
    col = jnp.arange(512)[None, :]
    part, hd = row // 128, row % 128
    valid = hd < N_HEADS
    sel = (jnp.where(valid & (col == hd * 128 + 64 + part), 1.0, 0.0)
           - jnp.where(valid & (col == hd * 128 + 67 + part), 1.0, 0.0))
    return sel.astype(BF16)


def _rows(v):
    return v.reshape(v.shape[0], 1, -1)


def kernel(x, c, positions, w_in, mu_shift, w0, w_decay_up, a0, w_aaa_up, w_gate_up, k_k, k_a, r_k,
           ln_x_g, ln_x_b, b_forget, q_norm_g, w_q_up, kv_norm_g, w_kv_up, w_branch_a, w_branch_b,
           w_branch_c, w_out, w_mod, b_mod, norm_mix_pre, norm_mix_post, norm_ffn_pre, norm_ffn_post,
           w_ffn_up, w_ffn_down):
    B, S, D = x.shape
    L = w_in.shape[0]
    tm = min(512, S)
    bf16 = lambda w: w.astype(BF16)

    c8 = jnp.pad(c, ((0, 8 - B), (0, 0)))
    mod = _modulation(c8, w_mod, b_mod)[:, :B].reshape(L, B, 6, D)
    cos_t, sin_t = _rope_tables(positions, tm)
    sel = _fox_selector()

    fq_scale = HEAD64 ** -0.5 * LOG2E
    mq_scale = (QK_NOPE_DIM + QK_ROPE_DIM) ** -0.5 * LOG2E
    w1, w_gates = _inproj_weights(w_in, fq_scale)
    bf = jnp.pad(_rows(b_forget), ((0, 0), (0, 0), (0, 128 - N_HEADS)))
    wq = _pad_heads(bf16(w_q_up * mq_scale), QK_NOPE_DIM + QK_ROPE_DIM, 256)
    wkv = bf16(w_kv_up)
    zeros64 = jnp.zeros((L, 64, RWKV_WIDTH), F32)
    wd = jnp.concatenate([w_decay_up, zeros64], axis=1)
    wa = jnp.concatenate([zeros64, w_aaa_up], axis=1)
    wba, wbb, wbc, wo = bf16(w_branch_a), bf16(w_branch_b), bf16(w_branch_c), bf16(w_out)
    wu, wdn = bf16(w_ffn_up), bf16(w_ffn_down)
    g_mix_pre, g_mix_post = _rows(norm_mix_pre), _rows(norm_mix_post)
    g_ffn_pre, g_ffn_post = _rows(norm_ffn_pre), _rows(norm_ffn_post)
    rwkv_rows = [_rows(v) for v in (mu_shift, w0)] + [wd, _rows(a0), wa, w_gate_up] + [
        _rows(v) for v in (k_k, k_a, r_k.reshape(L, -1), ln_x_g, ln_x_b)]
    qg, kvg = _rows(q_norm_g), _rows(kv_norm_g)

    for l in range(L):
        za, fq, fk, fv, mq, mk, mv = _inproj(l, x, mod, g_mix_pre, w1, bf, sel, qg, wq,
                                             kvg, wkv, cos_t, sin_t, tm)
        ya = _rwkv(l, za, *rwkv_rows, tm)
        yb = _attention(fq, fk, fv, HEAD64, "fox_attention")
        yc = _attention(mq, mk, mv, V_HEAD_DIM, "mla_attention")
        x = _merge(l, x, mod, g_mix_pre, g_mix_post, w_gates, ya, yb, yc, wba, wbb, wbc, wo, tm)
        x = _ffn(l, x, mod, g_ffn_pre, g_ffn_post, wu, wdn, tm)
    return x
```

```python
import functools

import jax
import jax.numpy as jnp
from jax import lax
from jax.experimental import pallas as pl
from jax.experimental.pallas import tpu as pltpu

F32 = jnp.float32
BF16 = jnp.bfloat16

D_MODEL = 1024
N_HEADS = 4
HEAD64 = 64
RWKV_WIDTH = N_HEADS * HEAD64
QK_NOPE_DIM = 128
QK_ROPE_DIM = 64
V_HEAD_DIM = 128
Q_LORA_RANK = 384
KV_LORA_RANK = 256
ROPE_THETA = 10000.0
D_FF = 4 * D_MODEL
NORM_EPS = 1e-6
RWKV_LN_EPS = 64e-5
MASK_VALUE = -1e30
LOG2E = 1.4426950408889634
EXP_NEG_HALF = 0.6065306597126334
FOX_Q_SCALE = HEAD64 ** -0.5 * LOG2E
CHUNK = 64

_FOX0 = 1024
_MLA0 = _FOX0 + 3 * 256 + N_HEADS
_GATE0 = _MLA0 + Q_LORA_RANK + KV_LORA_RANK + QK_ROPE_DIM

_C_RWKV = 0
_C_FQ = 1024
_C_FK = 1280
_C_FV = 1536
_C_FF = 1792
_C_QL = 1920
_C_KVL = 2304
_C_KPE = 2560
_C_END = 2688

VMEM_LIMIT = 56 * 1024 * 1024


def _bdot(a, b):
    return jnp.dot(a.astype(BF16), b.astype(BF16), preferred_element_type=F32)


def _bdot_nt(a, b):
    return lax.dot_general(a.astype(BF16), b.astype(BF16), (((1,), (1,)), ((), ())),
                           preferred_element_type=F32)


def _bdot_tn(a, b):
    return lax.dot_general(a.astype(BF16), b.astype(BF16), (((0,), (0,)), ((), ())),
                           preferred_element_type=F32)


def _split2_dot(a, b_bf16):
    hi = a.astype(BF16)
    lo = (a - hi.astype(F32)).astype(BF16)
    return (jnp.dot(hi, b_bf16, preferred_element_type=F32)
            + jnp.dot(lo, b_bf16, preferred_element_type=F32))


def _split3(a):
    hi = a.astype(BF16)
    r1 = a - hi.astype(F32)
    mid = r1.astype(BF16)
    lo = (r1 - mid.astype(F32)).astype(BF16)
    return hi, mid, lo


def _rms(x, gain):
    return x * lax.rsqrt(jnp.mean(x * x, axis=-1, keepdims=True) + NORM_EPS) * gain


def _softplus(y):
    return jnp.maximum(y, 0.0) + jnp.log(1.0 + jnp.exp(-jnp.abs(y)))


def _sigmoid(y):
    return 1.0 / (1.0 + jnp.exp(-y))


def _iota(shape, dim):
    return lax.broadcasted_iota(jnp.int32, shape, dim)


def _full(shape):
    return pl.BlockSpec(shape, lambda *_: (0,) * len(shape))


def _layer(arr, l):
    rest = arr.shape[1:]
    return pl.BlockSpec((None,) + rest, lambda *_: (l,) + (0,) * len(rest))


def _params(sem):
    return pltpu.CompilerParams(dimension_semantics=sem, vmem_limit_bytes=VMEM_LIMIT)


def _mod_kernel(c_ref, w_ref, b_ref, o_ref):
    c = c_ref[...]
    o_ref[...] = _split2_dot(c * _sigmoid(c), w_ref[...].astype(BF16)) + b_ref[...]


def _modulation(c8, w_mod, b_mod):
    L, D, D6 = w_mod.shape
    return pl.pallas_call(
        _mod_kernel,
        grid=(L, D6 // D),
        in_specs=[pl.BlockSpec((8, D), lambda l, j: (0, 0)),
                  pl.BlockSpec((None, D, D), lambda l, j: (l, 0, j)),
                  pl.BlockSpec((None, 1, D), lambda l, j: (l, 0, j))],
        out_specs=pl.BlockSpec((None, 8, D), lambda l, j: (l, 0, j)),
        out_shape=jax.ShapeDtypeStruct((L, 8, D6), F32),
        compiler_params=_params(("parallel", "parallel")),
        name="adaln_mod",
    )(c8, w_mod, b_mod.reshape(L, 1, D6))


def _rope_kernel(pos_ref, inv_ref, cos_ref, sin_ref):
    ang = pos_ref[...] * inv_ref[...]
    lane = _iota(ang.shape, 1)
    cos_ref[...] = jnp.where(lane < 64, jnp.cos(ang), 0.0)
    s = jnp.sin(ang)
    sin_ref[...] = jnp.where(lane < 32, -s, jnp.where(lane < 64, s, 0.0))


def _rope_tables(positions, tm):
    B, S = positions.shape
    inv_freq = ROPE_THETA ** (-jnp.arange(0, QK_ROPE_DIM, 2, dtype=F32) / QK_ROPE_DIM)
    inv128 = jnp.tile(inv_freq, 4).reshape(1, 128)
    pos = positions.astype(F32).reshape(B, S, 1)
    out = jax.ShapeDtypeStruct((B, S, 128), F32)
    return pl.pallas_call(
        _rope_kernel,
        grid=(B, S // tm),
        in_specs=[pl.BlockSpec((None, tm, 1), lambda b, i: (b, i, 0)), _full((1, 128))],
        out_specs=[pl.BlockSpec((None, tm, 128), lambda b, i: (b, i, 0))] * 2,
        out_shape=[out, out],
        compiler_params=_params(("parallel", "parallel")),
        name="rope_tables",
    )(pos, inv128)


def _rope128(x, cos_t, sin_t):
    lane = _iota(x.shape, 1)
    partner = jnp.where(lane < 32, pltpu.roll(x, 96, 1), pltpu.roll(x, 32, 1))
    return x * cos_t + partner * sin_t


def _inproj_kernel(x_ref, mod_ref, gpre_ref, w1_ref, bf_ref, sel_ref,
                   qg_ref, wq_ref, kvg_ref, wkv_ref, cos_ref, sin_ref,
                   za_ref, fq_ref, fk_ref, fv_ref, mq_ref, mk_ref, mv_ref, carry_ref):
    i = pl.program_id(1)
    tm = x_ref.shape[0]

    @pl.when(i == 0)
    def _():
        carry_ref[...] = jnp.zeros_like(carry_ref)

    x = x_ref[...]
    h = (_rms(x, gpre_ref[...]) * (1.0 + mod_ref[1:2, :]) + mod_ref[0:1, :]).astype(BF16)

    za_ref[...] = jnp.dot(h, w1_ref[:, _C_RWKV:_C_FQ], preferred_element_type=F32)

    f = jnp.dot(h, w1_ref[:, _C_FF:_C_QL], preferred_element_type=F32) + bf_ref[...]
    logf = -_softplus(-f)
    tri = (_iota((tm, tm), 0) >= _iota((tm, tm), 1)).astype(BF16)
    l_hi, l_mid, l_lo = _split3(logf)
    cum = (jnp.dot(tri, l_hi, preferred_element_type=F32)
           + jnp.dot(tri, l_mid, preferred_element_type=F32)
           + jnp.dot(tri, l_lo, preferred_element_type=F32)) + carry_ref[0:1, :]
    carry_ref[0:1, :] = cum[tm - 1:tm, :]
    cc = jnp.concatenate(_split3(cum * LOG2E), axis=1)
    aug = jnp.dot(cc, sel_ref[...], preferred_element_type=F32)
    fqk = jnp.dot(h, w1_ref[:, _C_FQ:_C_FV], preferred_element_type=F32)
    lane = _iota((1, 128), 1)
    q_ones = (lane >= 67) & (lane < 70)
    k_ones = (lane >= 64) & (lane < 67)
    fv_t = jnp.dot(h, w1_ref[:, _C_FV:_C_FF], preferred_element_type=F32).T
    ones_rows = (_iota((16, tm), 0) == 0).astype(BF16)
    for hd in range(N_HEADS):
        aug_h = aug[:, 128 * hd:128 * hd + 128]
        q_grp = fqk[:, 128 * (hd // 2):128 * (hd // 2) + 128] * FOX_Q_SCALE
        k_grp = fqk[:, 256 + 128 * (hd // 2):256 + 128 * (hd // 2) + 128]
        if hd % 2:
            q_grp = pltpu.roll(q_grp, 64, 1)
            k_grp = pltpu.roll(k_grp, 64, 1)
        fq_ref[hd] = jnp.where(lane < 64, q_grp, jnp.where(q_ones, 1.0, aug_h)).astype(BF16)
        fk_ref[hd] = jnp.where(lane < 64, k_grp, jnp.where(k_ones, 1.0, aug_h)).astype(BF16)
        fv_ref[hd] = jnp.concatenate([fv_t[64 * hd:64 * hd + 64].astype(BF16), ones_rows], axis=0)

    cos_t = cos_ref[...]
    sin_t = sin_ref[...]
    q_lat = jnp.dot(h, w1_ref[:, _C_QL:_C_KVL], preferred_element_type=F32)
    q = _bdot(_rms(q_lat, qg_ref[...]), wq_ref[...])
    kv_lat = jnp.dot(h, w1_ref[:, _C_KVL:_C_KPE], preferred_element_type=F32)
    kv = _bdot(_rms(kv_lat, kvg_ref[...]), wkv_ref[...])
    k_pe = _rope128(jnp.dot(h, w1_ref[:, _C_KPE:_C_END], preferred_element_type=F32), cos_t, sin_t)
    k_pe = k_pe.astype(BF16)
    for hd in range(N_HEADS):
        o = 256 * hd
        q_pe = _rope128(q[:, o + 128:o + 256], cos_t, sin_t)
        mq_ref[hd] = jnp.concatenate([q[:, o:o + 128], q_pe], axis=1).astype(BF16)
        mk_ref[hd] = jnp.concatenate([kv[:, o:o + 128].astype(BF16), k_pe], axis=1)
        mv_ref[hd] = jnp.concatenate([kv[:, o + 128:o + 256].T.astype(BF16), ones_rows], axis=0)


def _inproj(l, x, mod, gpre, w1, bf, sel, qg, wq, kvg, wkv, cos_t, sin_t, tm):
    B, S, D = x.shape
    H = N_HEADS
    tile = lambda w: pl.BlockSpec((None, tm, w), lambda b, i: (b, i, 0))
    head = lambda w: pl.BlockSpec((None, H, tm, w), lambda b, i: (b, 0, i, 0))
    hs = lambda w: jax.ShapeDtypeStruct((B, H, S, w), BF16)
    head_t = lambda r: pl.BlockSpec((None, H, None, r, tm), lambda b, i: (b, 0, i, 0, 0))
    hs_t = lambda r: jax.ShapeDtypeStruct((B, H, S // tm, r, tm), BF16)
    return pl.pallas_call(
        _inproj_kernel,
        grid=(B, S // tm),
        in_specs=[tile(D), pl.BlockSpec((None, None, 6, D), lambda b, i: (l, b, 0, 0)), _layer(gpre, l),
                  _layer(w1, l), _layer(bf, l), _full(sel.shape), _layer(qg, l), _layer(wq, l),
                  _layer(kvg, l), _layer(wkv, l), tile(128), tile(128)],
        out_specs=[tile(1024), head(128), head(128), head_t(80), head(256), head(256), head_t(144)],
        out_shape=[jax.ShapeDtypeStruct((B, S, 1024), F32), hs(128), hs(128), hs_t(80),
                   hs(256), hs(256), hs_t(144)],
        scratch_shapes=[pltpu.VMEM((8, 128), F32)],
        compiler_params=_params(("parallel", "arbitrary")),
        name="in_proj",
    )(x, mod, gpre, w1, bf, sel, qg, wq, kvg, wkv, cos_t, sin_t)


def _rwkv_kernel(z_ref, mu_ref, w0_ref, wd_ref, a0_ref, wa_ref, wg_ref, kk_ref, ka_ref, rk_ref,
                 lng_ref, lnb_ref, y_ref,
                 prev_ref, h_ref, rt_ref, at_ref, bt_ref, kt_ref, bh_ref, kh_ref, v_ref, gam_ref,
                 ys_ref, yp_ref, gw_ref, gw2_ref, pw_ref, rp_ref):
    i = pl.program_id(1)
    tm = z_ref.shape[0]
    W = RWKV_WIDTH
    C = CHUNK

    @pl.when(i == 0)
    def _():
        prev_ref[...] = jnp.zeros_like(prev_ref)
        h_ref[...] = jnp.zeros_like(h_ref)

    z = z_ref[...]
    zroll = pltpu.roll(z, 1, 0)
    zprev = jnp.concatenate([jnp.where(_iota((8, 1), 0) == 0, prev_ref[0:1, :], zroll[0:8]), zroll[8:]], axis=0)
    prev_ref[0:1, :] = z[tm - 1:tm, :]
    zm = z + (zprev - z) * mu_ref[...]
    r = zm[:, 0:W]
    k = zm[:, W:2 * W]
    v = zm[:, 2 * W:3 * W]
    lora_in = zm[:, 3 * W:3 * W + 128]
    gd = zm[:, 3 * W + 128:3 * W + 256]
    lw = -EXP_NEG_HALF * _sigmoid(w0_ref[...] + _bdot(jnp.tanh(lora_in), wd_ref[...]))
    a = _sigmoid(a0_ref[...] + _bdot(lora_in, wa_ref[...]))
    g = _bdot(_sigmoid(gd), wg_ref[...])
    seg = ((_iota((W, W), 0) // HEAD64) == (_iota((W, W), 1) // HEAD64)).astype(BF16)
    kk = k * kk_ref[...]
    kk = kk * lax.rsqrt(jnp.maximum(_bdot(kk * kk, seg), 1e-24))
    k2 = k * (1.0 + (a - 1.0) * ka_ref[...])

    tri = (_iota((C, C), 0) >= _iota((C, C), 1)).astype(BF16)
    lw3 = _split3(lw)
    cls, ces = [], []
    for c in range(tm // C):
        cl_c = sum(jnp.dot(tri, part[c * C:(c + 1) * C, :], preferred_element_type=F32) for part in lw3)
        cls.append(cl_c)
        ces.append(jnp.broadcast_to(cl_c[C - 1:C, :], (C, W)))
    cl = jnp.concatenate(cls, axis=0)
    ce = jnp.concatenate(ces, axis=0)
    e_neg = jnp.exp(-cl)
    e_end = jnp.exp(ce - cl)
    rt_ref[...] = r * jnp.exp(cl)
    at_ref[...] = -kk * jnp.exp(cl - lw)
    bt_ref[...] = kk * a * e_neg
    kt_ref[...] = k2 * e_neg
    bh_ref[...] = kk * a * e_end
    kh_ref[...] = k2 * e_end
    v_ref[...] = v
    gam_ref[...] = jnp.exp(ce)

    block_mask = (_iota((W, W), 0) // C) == (_iota((W, W), 1) // C)
    t_row = _iota((C, W), 0)
    s_col = _iota((C, W), 1) % C
    strict_lower = t_row > s_col
    incl_lower = t_row >= s_col
    eye_w = (t_row == s_col).astype(F32)

    def block(x):
        return jnp.where(block_mask, jnp.concatenate([x] * N_HEADS, axis=0), 0.0).astype(BF16)

    def fold(x_bd):
        return x_bd[0:C] + x_bd[C:2 * C] + x_bd[2 * C:3 * C] + x_bd[3 * C:4 * C]

    n_chunks = tm // C
    cs = range(n_chunks)
    rows = [slice(c * C, (c + 1) * C) for c in cs]
    a_n = [at_ref[rows[c], :] for c in cs]
    a_s = [block(a_n[c]) for c in cs]
    m = [_bdot_nt(jnp.concatenate([a_n[c], rt_ref[rows[c], :]], axis=0),
                  jnp.concatenate([block(bt_ref[rows[c], :]), block(kt_ref[rows[c], :])], axis=0))
         for c in cs]
    a_ab = [jnp.where(strict_lower, m[c][:C, :W], 0.0) for c in cs]
    m_rb = [jnp.where(incl_lower, m[c][C:, :W], 0.0) for c in cs]
    av = [_bdot(jnp.concatenate([jnp.where(strict_lower, m[c][:C, W:], 0.0),
                                 jnp.where(incl_lower, m[c][C:, W:], 0.0)], axis=0),
                block(v_ref[rows[c], :])) for c in cs]
    t_inv = [eye_w + a_ab[c] for c in cs]
    p = a_ab
    for _ in range(5):
        p = [_bdot(p[c], block(p[c])) for c in cs]
        t_inv = [t_inv[c] + _bdot(t_inv[c], block(p[c])) for c in cs]
    x = [_bdot(t_inv[c], jnp.concatenate([a_s[c], block(av[c][:C])], axis=1)) for c in cs]
    ry = [_bdot(m_rb[c], jnp.concatenate([block(x[c][:, :W]), block(x[c][:, W:])], axis=1)) for c in cs]
    for c in cs:
        rp_ref[rows[c], :] = (rt_ref[rows[c], :] + ry[c][:, :W]).astype(BF16)
        yp_ref[rows[c], :] = av[c][C:] + ry[c][:, W:]
        bh_n = bh_ref[rows[c], :]
        p_bd = jnp.where(block_mask, _bdot_tn(bh_n, x[c][:, :W]), 0.0)
        g_bd = jnp.where(block_mask,
                         _bdot_tn(jnp.concatenate([bh_n, kh_ref[rows[c], :]], axis=0),
                                  jnp.concatenate([x[c][:, W:], v_ref[rows[c], :]], axis=0)), 0.0)
        pw_ref[rows[c], :] = fold(p_bd)
        gw_ref[rows[c], :] = fold(g_bd)
        gam_row = gam_ref[c * C:c * C + 1, :]
        gw2_ref[rows[c], :] = _split2_dot(eye_w * gam_row, seg)

    def chunk_map(c):
        return gw2_ref[rows[c], :], pw_ref[rows[c], :], gw_ref[rows[c], :], gam_ref[c * C:c * C + 1, :]

    def compose(first, second):
        d1, p1, g1, row1 = first
        d2, p2, g2, row2 = second
        cross = _bdot(p2, jnp.concatenate([block(p1), block(g1)], axis=1))
        return (d2 * d1, d2 * p1 + p2 * row1 + cross[:, :W], d2 * g1 + cross[:, W:] + g2, row2 * row1)

    def apply(mp, h, ph):
        return mp[0] * h + ph + mp[2]

    singles = [chunk_map(c) for c in cs]
    pairs = [compose(singles[2 * q], singles[2 * q + 1]) for q in range(n_chunks // 2)]
    quads = [compose(pairs[2 * q], pairs[2 * q + 1]) for q in range(n_chunks // 4)]
    starts = {0: h_ref[...]}
    for c in cs:
        spans = [(1, singles[c])]
        if c % 2 == 0:
            spans.append((2, pairs[c // 2]))
        if c % 4 == 0:
            spans.append((4, quads[c // 4]))
        spans = [(n, mp) for n, mp in spans if c + n not in starts]
        lhs = jnp.concatenate([rp_ref[rows[c], :]] + [mp[1].astype(BF16) for _, mp in spans], axis=0)
        out = _bdot(lhs, block(starts[c]))
        ys_ref[rows[c], :] = out[:C] + yp_ref[rows[c], :]
        for q, (n, mp) in enumerate(spans):
            starts[c + n] = apply(mp, starts[c], out[(q + 1) * C:(q + 2) * C])
    h_ref[...] = starts[n_chunks]

    y = ys_ref[...]
    inv = 1.0 / HEAD64
    mean = _bdot(y, seg) * inv
    d = y - mean
    var = _bdot(d * d, seg) * inv
    yn = d * lax.rsqrt(var + RWKV_LN_EPS) * lng_ref[...] + lnb_ref[...]
    bonus = _bdot(r * k2 * rk_ref[...], seg) * v
    y_ref[...] = ((yn + bonus) * g).astype(BF16)


def _rwkv(l, za, mu, w0, wd, a0, wa, wg, k_k, k_a, r_k, ln_g, ln_b, tm):
    B, S, _ = za.shape
    W = RWKV_WIDTH
    smalls = [mu, w0, wd, a0, wa, wg, k_k, k_a, r_k, ln_g, ln_b]
    return pl.pallas_call(
        _rwkv_kernel,
        grid=(B, S // tm),
        in_specs=[pl.BlockSpec((None, tm, 1024), lambda b, i: (b, i, 0))] + [_layer(s, l) for s in smalls],
        out_specs=pl.BlockSpec((None, tm, W), lambda b, i: (b, i, 0)),
        out_shape=jax.ShapeDtypeStruct((B, S, W), BF16),
        scratch_shapes=[pltpu.VMEM((8, 1024), F32), pltpu.VMEM((CHUNK, W), F32)]
                       + [pltpu.VMEM((tm, W), F32)] * 13 + [pltpu.VMEM((tm, W), BF16)],
        compiler_params=_params(("parallel", "arbitrary")),
        name="rwkv7_chunked",
    )(za, *smalls)


def _attn_kernel(q_ref, k_ref, vt_ref, o_ref, acc_ref, m_ref, s_ref, *, dv):
    qi = pl.program_id(1)
    H, tq, _ = q_ref.shape
    tk = vt_ref.shape[-1]
    acc_ref[...] = jnp.zeros_like(acc_ref)
    m_ref[...] = jnp.full_like(m_ref, MASK_VALUE)

    def step(j, nblk, q0, qn, masked):
        nk = nblk * tk
        rows = pl.ds(pl.multiple_of(j * tk, tk), nk)
        lanes = slice(q0, q0 + qn)
        if masked:
            key_le_query = _iota((nk, qn), 0) <= _iota((nk, qn), 1)
        m_news, alphas = [], []
        for hd in range(H):
            st = _bdot_nt(k_ref[hd, rows, :], q_ref[hd, lanes, :])
            if masked:
                st = jnp.where(key_le_query, st, MASK_VALUE)
            s_ref[hd, 0:nk, lanes] = st
            m_old = m_ref[hd, 0:1, lanes]
            m_new = jnp.maximum(m_old, jnp.max(st, axis=0, keepdims=True))
            m_ref[hd, 0:1, lanes] = m_new
            m_news.append(m_new)
            alphas.append(jnp.exp2(m_old - m_new))
        for hd in range(H):
            p = jnp.exp2(s_ref[hd, 0:nk, lanes] - m_news[hd]).astype(BF16)
            pv = sum(jnp.dot(vt_ref[hd, j + b], p[b * tk:(b + 1) * tk], preferred_element_type=F32)
                     for b in range(nblk))
            acc_ref[hd, :, lanes] = alphas[hd] * acc_ref[hd, :, lanes] + pv

    blocks_per_tile = tq // tk

    def body(jj, carry):
        step(jj * blocks_per_tile, blocks_per_tile, 0, tq, False)
        return carry

    lax.fori_loop(0, qi, body, 0)
    first_diag = qi * blocks_per_tile
    step(first_diag, 1, 0, tq, True)
    step(first_diag + 1, 1, tk, tq - tk, True)
    outs = []
    for hd in range(H):
        acc = acc_ref[hd]
        outs.append(acc[:dv] * (1.0 / acc[dv:dv + 1]))
    o_ref[...] = jnp.concatenate(outs, axis=0).T.astype(BF16)


def _attention(q, k, vt, dv, name):
    B, H, S, dq = q.shape
    _, _, _, dva, tk = vt.shape
    tq = 2 * tk
    return pl.pallas_call(
        functools.partial(_attn_kernel, dv=dv),
        grid=(B, S // tq),
        in_specs=[pl.BlockSpec((None, H, tq, dq), lambda b, i: (b, 0, i, 0)),
                  pl.BlockSpec((None, H, S, dq), lambda b, i: (b, 0, 0, 0)),
                  pl.BlockSpec((None, H, S // tk, dva, tk), lambda b, i: (b, 0, 0, 0, 0))],
        out_specs=pl.BlockSpec((None, tq, H * dv), lambda b, i: (b, i, 0)),
        out_shape=jax.ShapeDtypeStruct((B, S, H * dv), BF16),
        scratch_shapes=[pltpu.VMEM((H, dva, tq), F32), pltpu.VMEM((H, 8, tq), F32),
                        pltpu.VMEM((H, tq, tq), F32)],
        compiler_params=_params(("parallel", "arbitrary")),
        name=name,
    )(q, k, vt)


def _merge_kernel(x_ref, mod_ref, gpre_ref, gpost_ref, wg_ref, ya_ref, yb_ref, yc_ref,
                  wa_ref, wb_ref, wc_ref, wo_ref, o_ref):
    D = D_MODEL
    x = x_ref[...]
    h = (_rms(x, gpre_ref[...]) * (1.0 + mod_ref[1:2, :]) + mod_ref[0:1, :]).astype(BF16)
    merged = None
    for j, (y_ref, w_ref) in enumerate(((ya_ref, wa_ref), (yb_ref, wb_ref), (yc_ref, wc_ref))):
        gate = _sigmoid(jnp.dot(h, wg_ref[:, j * D:(j + 1) * D], preferred_element_type=F32))
        term = gate * jnp.dot(y_ref[...], w_ref[...], preferred_element_type=F32)
        merged = term if merged is None else merged + term
    out = _bdot(merged, wo_ref[...])
    o_ref[...] = x + mod_ref[2:3, :] * _rms(out, gpost_ref[...])


def _merge(l, x, mod, gpre, gpost, wg, ya, yb, yc, wa, wb, wc, wo, tm):
    B, S, D = x.shape
    tile = lambda w: pl.BlockSpec((None, tm, w), lambda b, i: (b, i, 0))
    return pl.pallas_call(
        _merge_kernel,
        grid=(B, S // tm),
        in_specs=[tile(D), pl.BlockSpec((None, None, 6, D), lambda b, i: (l, b, 0, 0)), _layer(gpre, l),
                  _layer(gpost, l), _layer(wg, l), tile(ya.shape[-1]), tile(yb.shape[-1]),
                  tile(yc.shape[-1]), _layer(wa, l), _layer(wb, l), _layer(wc, l),
                  _layer(wo, l)],
        out_specs=tile(D),
        out_shape=jax.ShapeDtypeStruct((B, S, D), F32),
        compiler_params=_params(("parallel", "parallel")),
        name="merge_out",
    )(x, mod, gpre, gpost, wg, ya, yb, yc, wa, wb, wc, wo)


def _ffn_kernel(x_ref, mod_ref, gpre_ref, gpost_ref, wu_ref, wd_ref, o_ref):
    D = D_MODEL
    x = x_ref[...]
    h = (_rms(x, gpre_ref[...]) * (1.0 + mod_ref[4:5, :]) + mod_ref[3:4, :]).astype(BF16)
    acc = None
    for j in range(D_FF // D):
        u = jnp.maximum(jnp.dot(h, wu_ref[:, j * D:(j + 1) * D], preferred_element_type=F32), 0.0)
        term = jnp.dot((u * u).astype(BF16), wd_ref[j * D:(j + 1) * D, :], preferred_element_type=F32)
        acc = term if acc is None else acc + term
    o_ref[...] = x + mod_ref[5:6, :] * _rms(acc, gpost_ref[...])


def _ffn(l, x, mod, gpre, gpost, wu, wd, tm):
    B, S, D = x.shape
    tile = pl.BlockSpec((None, tm, D), lambda b, i: (b, i, 0))
    return pl.pallas_call(
        _ffn_kernel,
        grid=(B, S // tm),
        in_specs=[tile, pl.BlockSpec((None, None, 6, D), lambda b, i: (l, b, 0, 0)), _layer(gpre, l),
                  _layer(gpost, l), _layer(wu, l), _layer(wd, l)],
        out_specs=tile,
        out_shape=jax.ShapeDtypeStruct((B, S, D), F32),
        compiler_params=_params(("parallel", "parallel")),
        name="ffn",
    )(x, mod, gpre, gpost, wu, wd)


def _inproj_weights_kernel(w_ref, w1_ref, wg_ref):
    w = w_ref[...].astype(F32)
    rb = w.shape[0]
    zeros = lambda n: jnp.zeros((rb, n), F32)
    w1 = jnp.concatenate([
        w[:, :_FOX0 + 768],
        w[:, _FOX0 + 768:_MLA0], zeros(128 - N_HEADS),
        w[:, _MLA0:_MLA0 + Q_LORA_RANK + KV_LORA_RANK],
        w[:, _GATE0 - QK_ROPE_DIM:_GATE0], zeros(128 - QK_ROPE_DIM),
    ], axis=1)
    w1_ref[...] = w1.astype(BF16)
    wg_ref[...] = w[:, _GATE0:].astype(BF16)


def _inproj_weights(w_in, rb=256):
    L, D, n_in = w_in.shape
    n_gate = n_in - _GATE0
    return pl.pallas_call(
        _inproj_weights_kernel,
        grid=(L, D // rb),
        in_specs=[pl.BlockSpec((None, rb, n_in), lambda l, i: (l, i, 0))],
        out_specs=[pl.BlockSpec((None, rb, _C_END), lambda l, i: (l, i, 0)),
                   pl.BlockSpec((None, rb, n_gate), lambda l, i: (l, i, 0))],
        out_shape=[jax.ShapeDtypeStruct((L, D, _C_END), BF16),
                   jax.ShapeDtypeStruct((L, D, n_gate), BF16)],
        compiler_params=_params(("parallel", "parallel")),
        name="inproj_weights",
    )(w_in)


def _pad_heads(w, dh, to):
    L, K, _ = w.shape
    return jnp.pad(w.reshape(L, K, N_HEADS, dh),
                   ((0, 0), (0, 0), (0, 0), (0, to - dh))).reshape(L, K, N_HEADS * to)


def _fox_selector():
    row = jnp.arange(384)[:, None]
    col = jnp.arange(512)[None, :]
    part, hd = row // 128, row % 128
    valid = hd < N_HEADS
    sel = (jnp.where(valid & (col == hd * 128 + 64 + part), 1.0, 0.0)
           - jnp.where(valid & (col == hd * 128 + 67 + part), 1.0, 0.0))
    return sel.astype(BF16)


def _rows(v):
    return v.reshape(v.shape[0], 1, -1)


def kernel(x, c, positions, w_in, mu_shift, w0, w_decay_up, a0, w_aaa_up, w_gate_up, k_k, k_a, r_k,
           ln_x_g, ln_x_b, b_forget, q_norm_g, w_q_up, kv_norm_g, w_kv_up, w_branch_a, w_branch_b,
           w_branch_c, w_out, w_mod, b_mod, norm_mix_pre, norm_mix_post, norm_ffn_pre, norm_ffn_post,
           w_ffn_up, w_ffn_down):
    B, S, D = x.shape
    L = w_in.shape[0]
    tm = min(512, S)
    bf16 = lambda w: w.astype(BF16)

    c8 = jnp.pad(c, ((0, 8 - B), (0, 0)))
    mod = _modulation(c8, w_mod, b_mod)[:, :B].reshape(L, B, 6, D)
    cos_t, sin_t = _rope_tables(positions, tm)
    sel = _fox_selector()

    mq_scale = (QK_NOPE_DIM + QK_ROPE_DIM) ** -0.5 * LOG2E
    w1, w_gates = _inproj_weights(bf16(w_in))
    bf = jnp.pad(_rows(b_forget), ((0, 0), (0, 0), (0, 128 - N_HEADS)))
    wq = _pad_heads(bf16(w_q_up * mq_scale), QK_NOPE_DIM + QK_ROPE_DIM, 256)
    wkv = bf16(w_kv_up)
    zeros64 = jnp.zeros((L, 64, RWKV_WIDTH), F32)
    wd = jnp.concatenate([w_decay_up, zeros64], axis=1)
    wa = jnp.concatenate([zeros64, w_aaa_up], axis=1)
    wba, wbb, wbc, wo = bf16(w_branch_a), bf16(w_branch_b), bf16(w_branch_c), bf16(w_out)
    wu, wdn = bf16(w_ffn_up), bf16(w_ffn_down)
    g_mix_pre, g_mix_post = _rows(norm_mix_pre), _rows(norm_mix_post)
    g_ffn_pre, g_ffn_post = _rows(norm_ffn_pre), _rows(norm_ffn_post)
    rwkv_rows = [_rows(v) for v in (mu_shift, w0)] + [wd, _rows(a0), wa, w_gate_up] + [
        _rows(v) for v in (k_k, k_a, r_k.reshape(L, -1), ln_x_g, ln_x_b)]
    qg, kvg = _rows(q_norm_g), _rows(kv_norm_g)

    for l in range(L):
        za, fq, fk, fv, mq, mk, mv = _inproj(l, x, mod, g_mix_pre, w1, bf, sel, qg, wq,
                                             kvg, wkv, cos_t, sin_t, tm)
        ya = _rwkv(l, za, *rwkv_rows, tm)
        yb = _attention(fq, fk, fv, HEAD64, "fox_attention")
        yc = _attention(mq, mk, mv, V_HEAD_DIM, "mla_attention")
        x = _merge(l, x, mod, g_mix_pre, g_mix_post, w_gates, ya, yb, yc, wba, wbb, wbc, wo, tm)
        x = _ffn(l, x, mod, g_ffn_pre, g_ffn_post, wu, wdn, tm)
    return x
```

```python
import functools

import jax
import jax.numpy as jnp
from jax import lax
from jax.experimental import pallas as pl
from jax.experimental.pallas import tpu as pltpu

F32 = jnp.float32
BF16 = jnp.bfloat16

D_MODEL = 1024
N_HEADS = 4
HEAD64 = 64
RWKV_WIDTH = N_HEADS * HEAD64
QK_NOPE_DIM = 128
QK_ROPE_DIM = 64
V_HEAD_DIM = 128
Q_LORA_RANK = 384
KV_LORA_RANK = 256
ROPE_THETA = 10000.0
D_FF = 4 * D_MODEL
NORM_EPS = 1e-6
RWKV_LN_EPS = 64e-5
MASK_VALUE = -1e30
LOG2E = 1.4426950408889634
EXP_NEG_HALF = 0.6065306597126334
FOX_Q_SCALE = HEAD64 ** -0.5 * LOG2E
CHUNK = 64

_FOX0 = 1024
_MLA0 = _FOX0 + 3 * 256 + N_HEADS
_GATE0 = _MLA0 + Q_LORA_RANK + KV_LORA_RANK + QK_ROPE_DIM

_C_RWKV = 0
_C_FQ = 1024
_C_FK = 1280
_C_FV = 1536
_C_FF = 1792
_C_QL = 1920
_C_KVL = 2304
_C_KPE = 2560
_C_END = 2688

VMEM_LIMIT = 56 * 1024 * 1024


def _bdot(a, b):
    return jnp.dot(a.astype(BF16), b.astype(BF16), preferred_element_type=F32)


def _bdot_nt(a, b):
    return lax.dot_general(a.astype(BF16), b.astype(BF16), (((1,), (1,)), ((), ())),
                           preferred_element_type=F32)


def _bdot_tn(a, b):
    return lax.dot_general(a.astype(BF16), b.astype(BF16), (((0,), (0,)), ((), ())),
                           preferred_element_type=F32)


def _split2_dot(a, b_bf16):
    hi = a.astype(BF16)
    lo = (a - hi.astype(F32)).astype(BF16)
    return (jnp.dot(hi, b_bf16, preferred_element_type=F32)
            + jnp.dot(lo, b_bf16, preferred_element_type=F32))


def _split3(a):
    hi = a.astype(BF16)
    r1 = a - hi.astype(F32)
    mid = r1.astype(BF16)
    lo = (r1 - mid.astype(F32)).astype(BF16)
    return hi, mid, lo


def _rms(x, gain):
    return x * lax.rsqrt(jnp.mean(x * x, axis=-1, keepdims=True) + NORM_EPS) * gain


def _softplus(y):
    return jnp.maximum(y, 0.0) + jnp.log(1.0 + jnp.exp(-jnp.abs(y)))


def _sigmoid(y):
    return 1.0 / (1.0 + jnp.exp(-y))


def _iota(shape, dim):
    return lax.broadcasted_iota(jnp.int32, shape, dim)


def _full(shape):
    return pl.BlockSpec(shape, lambda *_: (0,) * len(shape))


def _layer(arr, l):
    rest = arr.shape[1:]
    return pl.BlockSpec((None,) + rest, lambda *_: (l,) + (0,) * len(rest))


def _params(sem):
    return pltpu.CompilerParams(dimension_semantics=sem, vmem_limit_bytes=VMEM_LIMIT)


def _mod_kernel(c_ref, w_ref, b_ref, o_ref):
    c = c_ref[...]
    o_ref[...] = _split2_dot(c * _sigmoid(c), w_ref[...].astype(BF16)) + b_ref[...]


def _modulation(c8, w_mod, b_mod):
    L, D, D6 = w_mod.shape
    return pl.pallas_call(
        _mod_kernel,
        grid=(L, D6 // D),
        in_specs=[pl.BlockSpec((8, D), lambda l, j: (0, 0)),
                  pl.BlockSpec((None, D, D), lambda l, j: (l, 0, j)),
                  pl.BlockSpec((None, 1, D), lambda l, j: (l, 0, j))],
        out_specs=pl.BlockSpec((None, 8, D), lambda l, j: (l, 0, j)),
        out_shape=jax.ShapeDtypeStruct((L, 8, D6), F32),
        compiler_params=_params(("parallel", "parallel")),
        name="adaln_mod",
    )(c8, w_mod, b_mod.reshape(L, 1, D6))


def _rope_kernel(pos_ref, inv_ref, cos_ref, sin_ref):
    ang = pos_ref[...] * inv_ref[...]
    lane = _iota(ang.shape, 1)
    cos_ref[...] = jnp.where(lane < 64, jnp.cos(ang), 0.0)
    s = jnp.sin(ang)
    sin_ref[...] = jnp.where(lane < 32, -s, jnp.where(lane < 64, s, 0.0))


def _rope_tables(positions, tm):
    B, S = positions.shape
    inv_freq = ROPE_THETA ** (-jnp.arange(0, QK_ROPE_DIM, 2, dtype=F32) / QK_ROPE_DIM)
    inv128 = jnp.tile(inv_freq, 4).reshape(1, 128)
    pos = positions.astype(F32).reshape(B, S, 1)
    out = jax.ShapeDtypeStruct((B, S, 128), F32)
    return pl.pallas_call(
        _rope_kernel,
        grid=(B, S // tm),
        in_specs=[pl.BlockSpec((None, tm, 1), lambda b, i: (b, i, 0)), _full((1, 128))],
        out_specs=[pl.BlockSpec((None, tm, 128), lambda b, i: (b, i, 0))] * 2,
        out_shape=[out, out],
        compiler_params=_params(("parallel", "parallel")),
        name="rope_tables",
    )(pos, inv128)


def _rope128(x, cos_t, sin_t):
    lane = _iota(x.shape, 1)
    partner = jnp.where(lane < 32, pltpu.roll(x, 96, 1), pltpu.roll(x, 32, 1))
    return x * cos_t + partner * sin_t


def _inproj_kernel(x_ref, mod_ref, gpre_ref, w1_ref, bf_ref, sel_ref,
                   qg_ref, wq_ref, kvg_ref, wkv_ref, cos_ref, sin_ref,
                   za_ref, fq_ref, fk_ref, fv_ref, mq_ref, mk_ref, mv_ref, carry_ref):
    i = pl.program_id(1)
    tm = x_ref.shape[0]

    @pl.when(i == 0)
    def _():
        carry_ref[...] = jnp.zeros_like(carry_ref)

    x = x_ref[...]
    h = (_rms(x, gpre_ref[...]) * (1.0 + mod_ref[1:2, :]) + mod_ref[0:1, :]).astype(BF16)

    za_ref[...] = jnp.dot(h, w1_ref[:, _C_RWKV:_C_FQ], preferred_element_type=F32)

    f = jnp.dot(h, w1_ref[:, _C_FF:_C_QL], preferred_element_type=F32) + bf_ref[...]
    logf = -_softplus(-f)
    tri = (_iota((tm, tm), 0) >= _iota((tm, tm), 1)).astype(BF16)
    l_hi, l_mid, l_lo = _split3(logf)
    cum = (jnp.dot(tri, l_hi, preferred_element_type=F32)
           + jnp.dot(tri, l_mid, preferred_element_type=F32)
           + jnp.dot(tri, l_lo, preferred_element_type=F32)) + carry_ref[0:1, :]
    carry_ref[0:1, :] = cum[tm - 1:tm, :]
    cc = jnp.concatenate(_split3(cum * LOG2E), axis=1)
    aug = jnp.dot(cc, sel_ref[...], preferred_element_type=F32)
    fqk = jnp.dot(h, w1_ref[:, _C_FQ:_C_FV], preferred_element_type=F32)
    lane = _iota((1, 128), 1)
    q_ones = (lane >= 67) & (lane < 70)
    k_ones = (lane >= 64) & (lane < 67)
    fv_t = jnp.dot(h, w1_ref[:, _C_FV:_C_FF], preferred_element_type=F32).T
    ones_rows = (_iota((16, tm), 0) == 0).astype(BF16)
    for hd in range(N_HEADS):
        aug_h = aug[:, 128 * hd:128 * hd + 128]
        q_grp = fqk[:, 128 * (hd // 2):128 * (hd // 2) + 128] * FOX_Q_SCALE
        k_grp = fqk[:, 256 + 128 * (hd // 2):256 + 128 * (hd // 2) + 128]
        if hd % 2:
            q_grp = pltpu.roll(q_grp, 64, 1)
            k_grp = pltpu.roll(k_grp, 64, 1)
        fq_ref[hd] = jnp.where(lane < 64, q_grp, jnp.where(q_ones, 1.0, aug_h)).astype(BF16)
        fk_ref[hd] = jnp.where(lane < 64, k_grp, jnp.where(k_ones, 1.0, aug_h)).astype(BF16)
        fv_ref[hd] = jnp.concatenate([fv_t[64 * hd:64 * hd + 64].astype(BF16), ones_rows], axis=0)

    cos_t = cos_ref[...]
    sin_t = sin_ref[...]
    q_lat = jnp.dot(h, w1_ref[:, _C_QL:_C_KVL], preferred_element_type=F32)
    q = _bdot(_rms(q_lat, qg_ref[...]), wq_ref[...])
    kv_lat = jnp.dot(h, w1_ref[:, _C_KVL:_C_KPE], preferred_element_type=F32)
    kv = _bdot(_rms(kv_lat, kvg_ref[...]), wkv_ref[...])
    k_pe = _rope128(jnp.dot(h, w1_ref[:, _C_KPE:_C_END], preferred_element_type=F32), cos_t, sin_t)
    k_pe = k_pe.astype(BF16)
    for hd in range(N_HEADS):
        o = 256 * hd
        q_pe = _rope128(q[:, o + 128:o + 256], cos_t, sin_t)
        mq_ref[hd] = jnp.concatenate([q[:, o:o + 128], q_pe], axis=1).astype(BF16)
        mk_ref[hd] = jnp.concatenate([kv[:, o:o + 128].astype(BF16), k_pe], axis=1)
        mv_ref[hd] = jnp.concatenate([kv[:, o + 128:o + 256].T.astype(BF16), ones_rows], axis=0)


def _inproj(l, x, mod, gpre, w1, bf, sel, qg, wq, kvg, wkv, cos_t, sin_t, tm):
    B, S, D = x.shape
    H = N_HEADS
    tile = lambda w: pl.BlockSpec((None, tm, w), lambda b, i: (b, i, 0))
    head = lambda w: pl.BlockSpec((None, H, tm, w), lambda b, i: (b, 0, i, 0))
    hs = lambda w: jax.ShapeDtypeStruct((B, H, S, w), BF16)
    head_t = lambda r: pl.BlockSpec((None, H, None, r, tm), lambda b, i: (b, 0, i, 0, 0))
    hs_t = lambda r: jax.ShapeDtypeStruct((B, H, S // tm, r, tm), BF16)
    return pl.pallas_call(
        _inproj_kernel,
        grid=(B, S // tm),
        in_specs=[tile(D), pl.BlockSpec((None, None, 6, D), lambda b, i: (l, b, 0, 0)), _layer(gpre, l),
                  _layer(w1, l), _layer(bf, l), _full(sel.shape), _layer(qg, l), _layer(wq, l),
                  _layer(kvg, l), _layer(wkv, l), tile(128), tile(128)],
        out_specs=[tile(1024), head(128), head(128), head_t(80), head(256), head(256), head_t(144)],
        out_shape=[jax.ShapeDtypeStruct((B, S, 1024), F32), hs(128), hs(128), hs_t(80),
                   hs(256), hs(256), hs_t(144)],
        scratch_shapes=[pltpu.VMEM((8, 128), F32)],
        compiler_params=_params(("parallel", "arbitrary")),
        name="in_proj",
    )(x, mod, gpre, w1, bf, sel, qg, wq, kvg, wkv, cos_t, sin_t)


def _rwkv_kernel(z_ref, mu_ref, w0_ref, wd_ref, a0_ref, wa_ref, wg_ref, kk_ref, ka_ref, rk_ref,
                 lng_ref, lnb_ref, y_ref,
                 prev_ref, h_ref, rt_ref, at_ref, bt_ref, kt_ref, bh_ref, kh_ref, v_ref, gam_ref,
                 ys_ref, yp_ref, gw_ref, gw2_ref, pw_ref, rp_ref):
    i = pl.program_id(1)
    tm = z_ref.shape[0]
    W = RWKV_WIDTH
    C = CHUNK

    @pl.when(i == 0)
    def _():
        prev_ref[...] = jnp.zeros_like(prev_ref)
        h_ref[...] = jnp.zeros_like(h_ref)

    z = z_ref[...]
    zroll = pltpu.roll(z, 1, 0)
    zprev = jnp.concatenate([jnp.where(_iota((8, 1), 0) == 0, prev_ref[0:1, :], zroll[0:8]), zroll[8:]], axis=0)
    prev_ref[0:1, :] = z[tm - 1:tm, :]
    zm = z + (zprev - z) * mu_ref[...]
    r = zm[:, 0:W]
    k = zm[:, W:2 * W]
    v = zm[:, 2 * W:3 * W]
    lora_in = zm[:, 3 * W:3 * W + 128]
    gd = zm[:, 3 * W + 128:3 * W + 256]
    lw = -EXP_NEG_HALF * _sigmoid(w0_ref[...] + _bdot(jnp.tanh(lora_in), wd_ref[...]))
    a = _sigmoid(a0_ref[...] + _bdot(lora_in, wa_ref[...]))
    g = _bdot(_sigmoid(gd), wg_ref[...])
    seg = ((_iota((W, W), 0) // HEAD64) == (_iota((W, W), 1) // HEAD64)).astype(BF16)
    kk = k * kk_ref[...]
    kk = kk * lax.rsqrt(jnp.maximum(_bdot(kk * kk, seg), 1e-24))
    k2 = k * (1.0 + (a - 1.0) * ka_ref[...])

    tri = (_iota((C, C), 0) >= _iota((C, C), 1)).astype(BF16)
    lw3 = _split3(lw)
    cls, ces = [], []
    for c in range(tm // C):
        cl_c = sum(jnp.dot(tri, part[c * C:(c + 1) * C, :], preferred_element_type=F32) for part in lw3)
        cls.append(cl_c)
        ces.append(jnp.broadcast_to(cl_c[C - 1:C, :], (C, W)))
    cl = jnp.concatenate(cls, axis=0)
    ce = jnp.concatenate(ces, axis=0)
    e_neg = jnp.exp(-cl)
    e_end = jnp.exp(ce - cl)
    rt_ref[...] = r * jnp.exp(cl)
    at_ref[...] = -kk * jnp.exp(cl - lw)
    bt_ref[...] = kk * a * e_neg
    kt_ref[...] = k2 * e_neg
    bh_ref[...] = kk * a * e_end
    kh_ref[...] = k2 * e_end
    v_ref[...] = v
    gam_ref[...] = jnp.exp(ce)

    block_mask = (_iota((W, W), 0) // C) == (_iota((W, W), 1) // C)
    t_row = _iota((C, W), 0)
    s_col = _iota((C, W), 1) % C
    strict_lower = t_row > s_col
    incl_lower = t_row >= s_col
    eye_w = (t_row == s_col).astype(F32)

    def block(x):
        return jnp.where(block_mask, jnp.concatenate([x] * N_HEADS, axis=0), 0.0).astype(BF16)

    def fold(x_bd):
        return x_bd[0:C] + x_bd[C:2 * C] + x_bd[2 * C:3 * C] + x_bd[3 * C:4 * C]

    n_chunks = tm // C
    cs = range(n_chunks)
    rows = [slice(c * C, (c + 1) * C) for c in cs]
    a_n = [at_ref[rows[c], :] for c in cs]
    a_s = [block(a_n[c]) for c in cs]
    m = [_bdot_nt(jnp.concatenate([a_n[c], rt_ref[rows[c], :]], axis=0),
                  jnp.concatenate([block(bt_ref[rows[c], :]), block(kt_ref[rows[c], :])], axis=0))
         for c in cs]
    a_ab = [jnp.where(strict_lower, m[c][:C, :W], 0.0) for c in cs]
    m_rb = [jnp.where(incl_lower, m[c][C:, :W], 0.0) for c in cs]
    av = [_bdot(jnp.concatenate([jnp.where(strict_lower, m[c][:C, W:], 0.0),
                                 jnp.where(incl_lower, m[c][C:, W:], 0.0)], axis=0),
                block(v_ref[rows[c], :])) for c in cs]
    t_inv = [eye_w + a_ab[c] for c in cs]
    p = a_ab
    for _ in range(5):
        p = [_bdot(p[c], block(p[c])) for c in cs]
        t_inv = [t_inv[c] + _bdot(t_inv[c], block(p[c])) for c in cs]
    x = [_bdot(t_inv[c], jnp.concatenate([a_s[c], block(av[c][:C])], axis=1)) for c in cs]
    ry = [_bdot(m_rb[c], jnp.concatenate([block(x[c][:, :W]), block(x[c][:, W:])], axis=1)) for c in cs]
    for c in cs:
        rp_ref[rows[c], :] = (rt_ref[rows[c], :] + ry[c][:, :W]).astype(BF16)
        yp_ref[rows[c], :] = av[c][C:] + ry[c][:, W:]
        bh_n = bh_ref[rows[c], :]
        p_bd = jnp.where(block_mask, _bdot_tn(bh_n, x[c][:, :W]), 0.0)
        g_bd = jnp.where(block_mask,
                         _bdot_tn(jnp.concatenate([bh_n, kh_ref[rows[c], :]], axis=0),
                                  jnp.concatenate([x[c][:, W:], v_ref[rows[c], :]], axis=0)), 0.0)
        pw_ref[rows[c], :] = fold(p_bd)
        gw_ref[rows[c], :] = fold(g_bd)
        gam_row = gam_ref[c * C:c * C + 1, :]
        gw2_ref[rows[c], :] = _split2_dot(eye_w * gam_row, seg)

    def chunk_map(c):
        return gw2_ref[rows[c], :], pw_ref[rows[c], :], gw_ref[rows[c], :], gam_ref[c * C:c * C + 1, :]

    def compose(first, second):
        d1, p1, g1, row1 = first
        d2, p2, g2, row2 = second
        cross = _bdot(p2, jnp.concatenate([block(p1), block(g1)], axis=1))
        return (d2 * d1, d2 * p1 + p2 * row1 + cross[:, :W], d2 * g1 + cross[:, W:] + g2, row2 * row1)

    def apply(mp, h, ph):
        return mp[0] * h + ph + mp[2]

    singles = [chunk_map(c) for c in cs]
    pairs = [compose(singles[2 * q], singles[2 * q + 1]) for q in range(n_chunks // 2)]
    quads = [compose(pairs[2 * q], pairs[2 * q + 1]) for q in range(n_chunks // 4)]
    starts = {0: h_ref[...]}
    for c in cs:
        spans = [(1, singles[c])]
        if c % 2 == 0:
            spans.append((2, pairs[c // 2]))
        if c % 4 == 0:
            spans.append((4, quads[c // 4]))
        spans = [(n, mp) for n, mp in spans if c + n not in starts]
        lhs = jnp.concatenate([rp_ref[rows[c], :]] + [mp[1].astype(BF16) for _, mp in spans], axis=0)
        out = _bdot(lhs, block(starts[c]))
        ys_ref[rows[c], :] = out[:C] + yp_ref[rows[c], :]
        for q, (n, mp) in enumerate(spans):
            starts[c + n] = apply(mp, starts[c], out[(q + 1) * C:(q + 2) * C])
    h_ref[...] = starts[n_chunks]

    y = ys_ref[...]
    inv = 1.0 / HEAD64
    mean = _bdot(y, seg) * inv
    d = y - mean
    var = _bdot(d * d, seg) * inv
    yn = d * lax.rsqrt(var + RWKV_LN_EPS) * lng_ref[...] + lnb_ref[...]
    bonus = _bdot(r * k2 * rk_ref[...], seg) * v
    y_ref[...] = ((yn + bonus) * g).astype(BF16)


def _rwkv(l, za, mu, w0, wd, a0, wa, wg, k_k, k_a, r_k, ln_g, ln_b, tm):
    B, S, _ = za.shape
    W = RWKV_WIDTH
    smalls = [mu, w0, wd, a0, wa, wg, k_k, k_a, r_k, ln_g, ln_b]
    return pl.pallas_call(
        _rwkv_kernel,
        grid=(B, S // tm),
        in_specs=[pl.BlockSpec((None, tm, 1024), lambda b, i: (b, i, 0))] + [_layer(s, l) for s in smalls],
        out_specs=pl.BlockSpec((None, tm, W), lambda b, i: (b, i, 0)),
        out_shape=jax.ShapeDtypeStruct((B, S, W), BF16),
        scratch_shapes=[pltpu.VMEM((8, 1024), F32), pltpu.VMEM((CHUNK, W), F32)]
                       + [pltpu.VMEM((tm, W), F32)] * 13 + [pltpu.VMEM((tm, W), BF16)],
        compiler_params=_params(("parallel", "arbitrary")),
        name="rwkv7_chunked",
    )(za, *smalls)


def _attn_kernel(q_ref, k_ref, vt_ref, o_ref, acc_ref, m_ref, s_ref, *, dv):
    qi = pl.program_id(1)
    H, tq, _ = q_ref.shape
    tk = vt_ref.shape[-1]
    acc_ref[...] = jnp.zeros_like(acc_ref)
    m_ref[...] = jnp.full_like(m_ref, MASK_VALUE)

    def step(j, nblk, q0, qn, masked):
        nk = nblk * tk
        rows = pl.ds(pl.multiple_of(j * tk, tk), nk)
        lanes = slice(q0, q0 + qn)
        if masked:
            key_le_query = _iota((nk, qn), 0) <= _iota((nk, qn), 1)
        m_news, alphas = [], []
        for hd in range(H):
            st = _bdot_nt(k_ref[hd, rows, :], q_ref[hd, lanes, :])
            if masked:
                st = jnp.where(key_le_query, st, MASK_VALUE)
            s_ref[hd, 0:nk, lanes] = st
            m_old = m_ref[hd, 0:1, lanes]
            m_new = jnp.maximum(m_old, jnp.max(st, axis=0, keepdims=True))
            m_ref[hd, 0:1, lanes] = m_new
            m_news.append(m_new)
            alphas.append(jnp.exp2(m_old - m_new))
        for hd in range(H):
            p = jnp.exp2(s_ref[hd, 0:nk, lanes] - m_news[hd]).astype(BF16)
            pv = sum(jnp.dot(vt_ref[hd, j + b], p[b * tk:(b + 1) * tk], preferred_element_type=F32)
                     for b in range(nblk))
            acc_ref[hd, :, lanes] = alphas[hd] * acc_ref[hd, :, lanes] + pv

    blocks_per_tile = tq // tk

    def body(jj, carry):
        step(jj * blocks_per_tile, blocks_per_tile, 0, tq, False)
        return carry

    lax.fori_loop(0, qi, body, 0)
    first_diag = qi * blocks_per_tile
    step(first_diag, 1, 0, tq, True)
    step(first_diag + 1, 1, tk, tq - tk, True)
    outs = []
    for hd in range(H):
        acc = acc_ref[hd]
        outs.append(acc[:dv] * (1.0 / acc[dv:dv + 1]))
    o_ref[...] = jnp.concatenate(outs, axis=0).T.astype(BF16)


def _attention(q, k, vt, dv, name):
    B, H, S, dq = q.shape
    _, _, _, dva, tk = vt.shape
    tq = 2 * tk
    return pl.pallas_call(
        functools.partial(_attn_kernel, dv=dv),
        grid=(B, S // tq),
        in_specs=[pl.BlockSpec((None, H, tq, dq), lambda b, i: (b, 0, i, 0)),
                  pl.BlockSpec((None, H, S, dq), lambda b, i: (b, 0, 0, 0)),
                  pl.BlockSpec((None, H, S // tk, dva, tk), lambda b, i: (b, 0, 0, 0, 0))],
        out_specs=pl.BlockSpec((None, tq, H * dv), lambda b, i: (b, i, 0)),
        out_shape=jax.ShapeDtypeStruct((B, S, H * dv), BF16),
        scratch_shapes=[pltpu.VMEM((H, dva, tq), F32), pltpu.VMEM((H, 8, tq), F32),
                        pltpu.VMEM((H, tq, tq), F32)],
        compiler_params=_params(("parallel", "arbitrary")),
        name=name,
    )(q, k, vt)


def _merge_kernel(x_ref, mod_ref, gpre_ref, gpost_ref, wg_ref, ya_ref, yb_ref, yc_ref,
                  wa_ref, wb_ref, wc_ref, wo_ref, o_ref):
    D = D_MODEL
    x = x_ref[...]
    h = (_rms(x, gpre_ref[...]) * (1.0 + mod_ref[1:2, :]) + mod_ref[0:1, :]).astype(BF16)
    merged = None
    for j, (y_ref, w_ref) in enumerate(((ya_ref, wa_ref), (yb_ref, wb_ref), (yc_ref, wc_ref))):
        gate = _sigmoid(jnp.dot(h, wg_ref[:, j * D:(j + 1) * D], preferred_element_type=F32))
        term = gate * jnp.dot(y_ref[...], w_ref[...], preferred_element_type=F32)
        merged = term if merged is None else merged + term
    out = _bdot(merged, wo_ref[...])
    o_ref[...] = x + mod_ref[2:3, :] * _rms(out, gpost_ref[...])


def _merge(l, x, mod, gpre, gpost, wg, ya, yb, yc, wa, wb, wc, wo, tm):
    B, S, D = x.shape
    tile = lambda w: pl.BlockSpec((None, tm, w), lambda b, i: (b, i, 0))
    return pl.pallas_call(
        _merge_kernel,
        grid=(B, S // tm),
        in_specs=[tile(D), pl.BlockSpec((None, None, 6, D), lambda b, i: (l, b, 0, 0)), _layer(gpre, l),
                  _layer(gpost, l), _layer(wg, l), tile(ya.shape[-1]), tile(yb.shape[-1]),
                  tile(yc.shape[-1]), _layer(wa, l), _layer(wb, l), _layer(wc, l),
                  _layer(wo, l)],
        out_specs=tile(D),
        out_shape=jax.ShapeDtypeStruct((B, S, D), F32),
        compiler_params=_params(("parallel", "parallel")),
        name="merge_out",
    )(x, mod, gpre, gpost, wg, ya, yb, yc, wa, wb, wc, wo)


def _ffn_kernel(x_ref, mod_ref, gpre_ref, gpost_ref, wu_ref, wd_ref, o_ref):
    D = D_MODEL
    x = x_ref[...]
    h = (_rms(x, gpre_ref[...]) * (1.0 + mod_ref[4:5, :]) + mod_ref[3:4, :]).astype(BF16)
    acc = None
    for j in range(D_FF // D):
        u = jnp.maximum(jnp.dot(h, wu_ref[:, j * D:(j + 1) * D], preferred_element_type=F32), 0.0)
        term = jnp.dot((u * u).astype(BF16), wd_ref[j * D:(j + 1) * D, :], preferred_element_type=F32)
        acc = term if acc is None else acc + term
    o_ref[...] = x + mod_ref[5:6, :] * _rms(acc, gpost_ref[...])


def _ffn(l, x, mod, gpre, gpost, wu, wd, tm):
    B, S, D = x.shape
    tile = pl.BlockSpec((None, tm, D), lambda b, i: (b, i, 0))
    return pl.pallas_call(
        _ffn_kernel,
        grid=(B, S // tm),
        in_specs=[tile, pl.BlockSpec((None, None, 6, D), lambda b, i: (l, b, 0, 0)), _layer(gpre, l),
                  _layer(gpost, l), _layer(wu, l), _layer(wd, l)],
        out_specs=tile,
        out_shape=jax.ShapeDtypeStruct((B, S, D), F32),
        compiler_params=_params(("parallel", "parallel")),
        name="ffn",
    )(x, mod, gpre, gpost, wu, wd)


def _inproj_weights_kernel(w_ref, w1_ref, wg_ref):
    w = w_ref[...].astype(F32)
    rb = w.shape[0]
    zeros = lambda n: jnp.zeros((rb, n), F32)
    w1 = jnp.concatenate([
        w[:, :_FOX0 + 768],
        w[:, _FOX0 + 768:_MLA0], zeros(128 - N_HEADS),
        w[:, _MLA0:_MLA0 + Q_LORA_RANK + KV_LORA_RANK],
        w[:, _GATE0 - QK_ROPE_DIM:_GATE0], zeros(128 - QK_ROPE_DIM),
    ], axis=1)
    w1_ref[...] = w1.astype(BF16)
    wg_ref[...] = w[:, _GATE0:].astype(BF16)


def _inproj_weights(w_in, rb=256):
    L, D, n_in = w_in.shape
    n_gate = n_in - _GATE0
    return pl.pallas_call(
        _inproj_weights_kernel,
        grid=(L, D // rb),
        in_specs=[pl.BlockSpec((None, rb, n_in), lambda l, i: (l, i, 0))],
        out_specs=[pl.BlockSpec((None, rb, _C_END), lambda l, i: (l, i, 0)),
                   pl.BlockSpec((None, rb, n_gate), lambda l, i: (l, i, 0))],
        out_shape=[jax.ShapeDtypeStruct((L, D, _C_END), BF16),
                   jax.ShapeDtypeStruct((L, D, n_gate), BF16)],
        compiler_params=_params(("parallel", "parallel")),
        name="inproj_weights",
    )(w_in)


def _pad_heads(w, dh, to):
    L, K, _ = w.shape
    return jnp.pad(w.reshape(L, K, N_HEADS, dh),
                   ((0, 0), (0, 0), (0, 0), (0, to - dh))).reshape(L, K, N_HEADS * to)


def _fox_selector():
    row = jnp.arange(384)[:, None]
    col = jnp.arange(512)[None, :]
    part, hd = row // 128, row % 128
    valid = hd < N_HEADS
    sel = (jnp.where(valid & (col == hd * 128 + 64 + part), 1.0, 0.0)
           - jnp.where(valid & (col == hd * 128 + 67 + part), 1.0, 0.0))
    return sel.astype(BF16)


def _rows(v):
    return v.reshape(v.shape[0], 1, -1)


def kernel(x, c, positions, w_in, mu_shift, w0, w_decay_up, a0, w_aaa_up, w_gate_up, k_k, k_a, r_k,
           ln_x_g, ln_x_b, b_forget, q_norm_g, w_q_up, kv_norm_g, w_kv_up, w_branch_a, w_branch_b,
           w_branch_c, w_out, w_mod, b_mod, norm_mix_pre, norm_mix_post, norm_ffn_pre, norm_ffn_post,
           w_ffn_up, w_ffn_down):
    B, S, D = x.shape
    L = w_in.shape[0]
    tm = min(512, S)
    bf16 = lambda w: w.astype(BF16)

    c8 = jnp.pad(c, ((0, 8 - B), (0, 0)))
    mod = _modulation(c8, w_mod, b_mod)[:, :B].reshape(L, B, 6, D)
    cos_t, sin_t = _rope_tables(positions, tm)
    sel = _fox_selector()

    mq_scale = (QK_NOPE_DIM + QK_ROPE_DIM) ** -0.5 * LOG2E
    w1, w_gates = _inproj_weights(bf16(w_in))
    bf = jnp.pad(_rows(b_forget), ((0, 0), (0, 0), (0, 128 - N_HEADS)))
    wq = _pad_heads(bf16(w_q_up * mq_scale), QK_NOPE_DIM + QK_ROPE_DIM, 256)
    wkv = bf16(w_kv_up)
    zeros64 = jnp.zeros((L, 64, RWKV_WIDTH), F32)
    wd = jnp.concatenate([w_decay_up, zeros64], axis=1)
    wa = jnp.concatenate([zeros64, w_aaa_up], axis=1)
    wba, wbb, wbc, wo = bf16(w_branch_a), bf16(w_branch_b), bf16(w_branch_c), bf16(w_out)
    wu, wdn = bf16(w_ffn_up), bf16(w_ffn_down)
    g_mix_pre, g_mix_post = _rows(norm_mix_pre), _rows(norm_mix_post)
    g_ffn_pre, g_ffn_post = _rows(norm_ffn_pre), _rows(norm_ffn_post)
    rwkv_rows = [_rows(v) for v in (mu_shift, w0)] + [wd, _rows(a0), wa, w_gate_up] + [
        _rows(v) for v in (k_k, k_a, r_k.reshape(L, -1), ln_x_g, ln_x_b)]
    qg, kvg = _rows(q_norm_g), _rows(kv_norm_g)

    for l in range(L):
        za, fq, fk, fv, mq, mk, mv = _inproj(l, x, mod, g_mix_pre, w1, bf, sel, qg, wq,
                                             kvg, wkv, cos_t, sin_t, tm)
        ya = _rwkv(l, za, *rwkv_rows, min(2 * tm, S))
        yb = _attention(fq, fk, fv, HEAD64, "fox_attention")
        yc = _attention(mq, mk, mv, V_HEAD_DIM, "mla_attention")
        x = _merge(l, x, mod, g_mix_pre, g_mix_post, w_gates, ya, yb, yc, wba, wbb, wbc, wo, tm)
        x = _ffn(l, x, mod, g_ffn_pre, g_ffn_post, wu, wdn, tm)
    return x
```

```python
import functools

import jax
import jax.numpy as jnp
from jax import lax
from jax.experimental import pallas as pl
from jax.experimental.pallas import tpu as pltpu

F32 = jnp.float32
BF16 = jnp.bfloat16

D_MODEL = 1024
N_HEADS = 4
HEAD64 = 64
RWKV_WIDTH = N_HEADS * HEAD64
QK_NOPE_DIM = 128
QK_ROPE_DIM = 64
V_HEAD_DIM = 128
Q_LORA_RANK = 384
KV_LORA_RANK = 256
ROPE_THETA = 10000.0
D_FF = 4 * D_MODEL
NORM_EPS = 1e-6
RWKV_LN_EPS = 64e-5
MASK_VALUE = -1e30
LOG2E = 1.4426950408889634
EXP_NEG_HALF = 0.6065306597126334
FOX_Q_SCALE = HEAD64 ** -0.5 * LOG2E
CHUNK = 64

_FOX0 = 1024
_MLA0 = _FOX0 + 3 * 256 + N_HEADS
_GATE0 = _MLA0 + Q_LORA_RANK + KV_LORA_RANK + QK_ROPE_DIM

_C_RWKV = 0
_C_FQ = 1024
_C_FK = 1280
_C_FV = 1536
_C_FF = 1792
_C_QL = 1920
_C_KVL = 2304
_C_KPE = 2560
_C_END = 2688

VMEM_LIMIT = 56 * 1024 * 1024


def _bdot(a, b):
    return jnp.dot(a.astype(BF16), b.astype(BF16), preferred_element_type=F32)


def _bdot_nt(a, b):
    return lax.dot_general(a.astype(BF16), b.astype(BF16), (((1,), (1,)), ((), ())),
                           preferred_element_type=F32)


def _bdot_tn(a, b):
    return lax.dot_general(a.astype(BF16), b.astype(BF16), (((0,), (0,)), ((), ())),
                           preferred_element_type=F32)


def _split2_dot(a, b_bf16):
    hi = a.astype(BF16)
    lo = (a - hi.astype(F32)).astype(BF16)
    return (jnp.dot(hi, b_bf16, preferred_element_type=F32)
            + jnp.dot(lo, b_bf16, preferred_element_type=F32))


def _split3(a):
    hi = a.astype(BF16)
    r1 = a - hi.astype(F32)
    mid = r1.astype(BF16)
    lo = (r1 - mid.astype(F32)).astype(BF16)
    return hi, mid, lo


def _rms(x, gain):
    return x * lax.rsqrt(jnp.mean(x * x, axis=-1, keepdims=True) + NORM_EPS) * gain


def _softplus(y):
    return jnp.maximum(y, 0.0) + jnp.log(1.0 + jnp.exp(-jnp.abs(y)))


def _sigmoid(y):
    return 1.0 / (1.0 + jnp.exp(-y))


def _iota(shape, dim):
    return lax.broadcasted_iota(jnp.int32, shape, dim)


def _full(shape):
    return pl.BlockSpec(shape, lambda *_: (0,) * len(shape))


def _layer(arr, l):
    rest = arr.shape[1:]
    return pl.BlockSpec((None,) + rest, lambda *_: (l,) + (0,) * len(rest))


def _params(sem):
    return pltpu.CompilerParams(dimension_semantics=sem, vmem_limit_bytes=VMEM_LIMIT)


def _mod_kernel(c_ref, w_ref, b_ref, o_ref):
    c = c_ref[...]
    o_ref[...] = _split2_dot(c * _sigmoid(c), w_ref[...].astype(BF16)) + b_ref[...]


def _modulation(c8, w_mod, b_mod):
    L, D, D6 = w_mod.shape
    return pl.pallas_call(
        _mod_kernel,
        grid=(L, D6 // D),
        in_specs=[pl.BlockSpec((8, D), lambda l, j: (0, 0)),
                  pl.BlockSpec((None, D, D), lambda l, j: (l, 0, j)),
                  pl.BlockSpec((None, 1, D), lambda l, j: (l, 0, j))],
        out_specs=pl.BlockSpec((None, 8, D), lambda l, j: (l, 0, j)),
        out_shape=jax.ShapeDtypeStruct((L, 8, D6), F32),
        compiler_params=_params(("parallel", "parallel")),
        name="adaln_mod",
    )(c8, w_mod, b_mod.reshape(L, 1, D6))


def _rope_kernel(pos_ref, inv_ref, cos_ref, sin_ref):
    ang = pos_ref[...] * inv_ref[...]
    lane = _iota(ang.shape, 1)
    cos_ref[...] = jnp.where(lane < 64, jnp.cos(ang), 0.0)
    s = jnp.sin(ang)
    sin_ref[...] = jnp.where(lane < 32, -s, jnp.where(lane < 64, s, 0.0))


def _rope_tables(positions, tm):
    B, S = positions.shape
    inv_freq = ROPE_THETA ** (-jnp.arange(0, QK_ROPE_DIM, 2, dtype=F32) / QK_ROPE_DIM)
    inv128 = jnp.tile(inv_freq, 4).reshape(1, 128)
    pos = positions.astype(F32).reshape(B, S, 1)
    out = jax.ShapeDtypeStruct((B, S, 128), F32)
    return pl.pallas_call(
        _rope_kernel,
        grid=(B, S // tm),
        in_specs=[pl.BlockSpec((None, tm, 1), lambda b, i: (b, i, 0)), _full((1, 128))],
        out_specs=[pl.BlockSpec((None, tm, 128), lambda b, i: (b, i, 0))] * 2,
        out_shape=[out, out],
        compiler_params=_params(("parallel", "parallel")),
        name="rope_tables",
    )(pos, inv128)


def _rope128(x, cos_t, sin_t):
    lane = _iota(x.shape, 1)
    partner = jnp.where(lane < 32, pltpu.roll(x, 96, 1), pltpu.roll(x, 32, 1))
    return x * cos_t + partner * sin_t


def _inproj_kernel(x_ref, mod_ref, gpre_ref, w1_ref, bf_ref, sel_ref,
                   qg_ref, wq_ref, kvg_ref, wkv_ref, cos_ref, sin_ref,
                   za_ref, fq_ref, fk_ref, fv_ref, mq_ref, mk_ref, mv_ref, carry_ref):
    i = pl.program_id(1)
    tm = x_ref.shape[0]

    @pl.when(i == 0)
    def _():
        carry_ref[...] = jnp.zeros_like(carry_ref)

    x = x_ref[...]
    h = (_rms(x, gpre_ref[...]) * (1.0 + mod_ref[1:2, :]) + mod_ref[0:1, :]).astype(BF16)

    za_ref[...] = jnp.dot(h, w1_ref[:, _C_RWKV:_C_FQ], preferred_element_type=F32)

    f = jnp.dot(h, w1_ref[:, _C_FF:_C_QL], preferred_element_type=F32) + bf_ref[...]
    logf = -_softplus(-f)
    tri = (_iota((tm, tm), 0) >= _iota((tm, tm), 1)).astype(BF16)
    l_hi, l_mid, l_lo = _split3(logf)
    cum = (jnp.dot(tri, l_hi, preferred_element_type=F32)
           + jnp.dot(tri, l_mid, preferred_element_type=F32)
           + jnp.dot(tri, l_lo, preferred_element_type=F32)) + carry_ref[0:1, :]
    carry_ref[0:1, :] = cum[tm - 1:tm, :]
    cc = jnp.concatenate(_split3(cum * LOG2E), axis=1)
    aug = jnp.dot(cc, sel_ref[...], preferred_element_type=F32)
    fqk = jnp.dot(h, w1_ref[:, _C_FQ:_C_FV], preferred_element_type=F32)
    lane = _iota((1, 128), 1)
    q_ones = (lane >= 67) & (lane < 70)
    k_ones = (lane >= 64) & (lane < 67)
    fv_t = jnp.dot(h, w1_ref[:, _C_FV:_C_FF], preferred_element_type=F32).T
    ones_rows = (_iota((16, tm), 0) == 0).astype(BF16)
    for hd in range(N_HEADS):
        aug_h = aug[:, 128 * hd:128 * hd + 128]
        q_grp = fqk[:, 128 * (hd // 2):128 * (hd // 2) + 128] * FOX_Q_SCALE
        k_grp = fqk[:, 256 + 128 * (hd // 2):256 + 128 * (hd // 2) + 128]
        if hd % 2:
            q_grp = pltpu.roll(q_grp, 64, 1)
            k_grp = pltpu.roll(k_grp, 64, 1)
        fq_ref[hd] = jnp.where(lane < 64, q_grp, jnp.where(q_ones, 1.0, aug_h)).astype(BF16)
        fk_ref[hd] = jnp.where(lane < 64, k_grp, jnp.where(k_ones, 1.0, aug_h)).astype(BF16)
        fv_ref[hd] = jnp.concatenate([fv_t[64 * hd:64 * hd + 64].astype(BF16), ones_rows], axis=0)

    cos_t = cos_ref[...]
    sin_t = sin_ref[...]
    q_lat = jnp.dot(h, w1_ref[:, _C_QL:_C_KVL], preferred_element_type=F32)
    q = _bdot(_rms(q_lat, qg_ref[...]), wq_ref[...])
    kv_lat = jnp.dot(h, w1_ref[:, _C_KVL:_C_KPE], preferred_element_type=F32)
    kv = _bdot(_rms(kv_lat, kvg_ref[...]), wkv_ref[...])
    k_pe = _rope128(jnp.dot(h, w1_ref[:, _C_KPE:_C_END], preferred_element_type=F32), cos_t, sin_t)
    k_pe = k_pe.astype(BF16)
    for hd in range(N_HEADS):
        o = 256 * hd
        q_pe = _rope128(q[:, o + 128:o + 256], cos_t, sin_t)
        mq_ref[hd] = jnp.concatenate([q[:, o:o + 128], q_pe], axis=1).astype(BF16)
        mk_ref[hd] = jnp.concatenate([kv[:, o:o + 128].astype(BF16), k_pe], axis=1)
        mv_ref[hd] = jnp.concatenate([kv[:, o + 128:o + 256].T.astype(BF16), ones_rows], axis=0)


def _inproj(l, x, mod, gpre, w1, bf, sel, qg, wq, kvg, wkv, cos_t, sin_t, tm):
    B, S, D = x.shape
    H = N_HEADS
    tile = lambda w: pl.BlockSpec((None, tm, w), lambda b, i: (b, i, 0))
    head = lambda w: pl.BlockSpec((None, H, tm, w), lambda b, i: (b, 0, i, 0))
    hs = lambda w: jax.ShapeDtypeStruct((B, H, S, w), BF16)
    head_t = lambda r: pl.BlockSpec((None, H, None, r, tm), lambda b, i: (b, 0, i, 0, 0))
    hs_t = lambda r: jax.ShapeDtypeStruct((B, H, S // tm, r, tm), BF16)
    return pl.pallas_call(
        _inproj_kernel,
        grid=(B, S // tm),
        in_specs=[tile(D), pl.BlockSpec((None, None, 6, D), lambda b, i: (l, b, 0, 0)), _layer(gpre, l),
                  _layer(w1, l), _layer(bf, l), _full(sel.shape), _layer(qg, l), _layer(wq, l),
                  _layer(kvg, l), _layer(wkv, l), tile(128), tile(128)],
        out_specs=[tile(1024), head(128), head(128), head_t(80), head(256), head(256), head_t(144)],
        out_shape=[jax.ShapeDtypeStruct((B, S, 1024), F32), hs(128), hs(128), hs_t(80),
                   hs(256), hs(256), hs_t(144)],
        scratch_shapes=[pltpu.VMEM((8, 128), F32)],
        compiler_params=_params(("parallel", "arbitrary")),
        name="in_proj",
    )(x, mod, gpre, w1, bf, sel, qg, wq, kvg, wkv, cos_t, sin_t)


def _rwkv_kernel(z_ref, mu_ref, w0_ref, wd_ref, a0_ref, wa_ref, wg_ref, kk_ref, ka_ref, rk_ref,
                 lng_ref, lnb_ref, y_ref,
                 prev_ref, h_ref, rt_ref, at_ref, bt_ref, kt_ref, bh_ref, kh_ref, v_ref, gam_ref,
                 ys_ref, yp_ref, gw_ref, gw2_ref, pw_ref, rp_ref):
    i = pl.program_id(1)
    tm = z_ref.shape[0]
    W = RWKV_WIDTH
    C = CHUNK

    @pl.when(i == 0)
    def _():
        prev_ref[...] = jnp.zeros_like(prev_ref)
        h_ref[...] = jnp.zeros_like(h_ref)

    z = z_ref[...]
    zroll = pltpu.roll(z, 1, 0)
    zprev = jnp.concatenate([jnp.where(_iota((8, 1), 0) == 0, prev_ref[0:1, :], zroll[0:8]), zroll[8:]], axis=0)
    prev_ref[0:1, :] = z[tm - 1:tm, :]
    zm = z + (zprev - z) * mu_ref[...]
    r = zm[:, 0:W]
    k = zm[:, W:2 * W]
    v = zm[:, 2 * W:3 * W]
    lora_in = zm[:, 3 * W:3 * W + 128]
    gd = zm[:, 3 * W + 128:3 * W + 256]
    lw = -EXP_NEG_HALF * _sigmoid(w0_ref[...] + _bdot(jnp.tanh(lora_in), wd_ref[...]))
    a = _sigmoid(a0_ref[...] + _bdot(lora_in, wa_ref[...]))
    g = _bdot(_sigmoid(gd), wg_ref[...])
    seg = ((_iota((W, W), 0) // HEAD64) == (_iota((W, W), 1) // HEAD64)).astype(BF16)
    kk = k * kk_ref[...]
    kk = kk * lax.rsqrt(jnp.maximum(_bdot(kk * kk, seg), 1e-24))
    k2 = k * (1.0 + (a - 1.0) * ka_ref[...])

    tri = (_iota((C, C), 0) >= _iota((C, C), 1)).astype(BF16)
    lw3 = _split3(lw)
    cls, ces = [], []
    for c in range(tm // C):
        cl_c = sum(jnp.dot(tri, part[c * C:(c + 1) * C, :], preferred_element_type=F32) for part in lw3)
        cls.append(cl_c)
        ces.append(jnp.broadcast_to(cl_c[C - 1:C, :], (C, W)))
    cl = jnp.concatenate(cls, axis=0)
    ce = jnp.concatenate(ces, axis=0)
    e_neg = jnp.exp(-cl)
    e_end = jnp.exp(ce - cl)
    rt_ref[...] = r * jnp.exp(cl)
    at_ref[...] = -kk * jnp.exp(cl - lw)
    bt_ref[...] = kk * a * e_neg
    kt_ref[...] = k2 * e_neg
    bh_ref[...] = kk * a * e_end
    kh_ref[...] = k2 * e_end
    v_ref[...] = v
    gam_ref[...] = jnp.exp(ce)

    block_mask = (_iota((W, W), 0) // C) == (_iota((W, W), 1) // C)
    t_row = _iota((C, W), 0)
    s_col = _iota((C, W), 1) % C
    strict_lower = t_row > s_col
    incl_lower = t_row >= s_col
    eye_w = (t_row == s_col).astype(F32)

    def block(x):
        return jnp.where(block_mask, jnp.concatenate([x] * N_HEADS, axis=0), 0.0).astype(BF16)

    def fold(x_bd):
        return x_bd[0:C] + x_bd[C:2 * C] + x_bd[2 * C:3 * C] + x_bd[3 * C:4 * C]

    n_chunks = tm // C
    cs = range(n_chunks)
    rows = [slice(c * C, (c + 1) * C) for c in cs]
    a_n = [at_ref[rows[c], :] for c in cs]
    a_s = [block(a_n[c]) for c in cs]
    m = [_bdot_nt(jnp.concatenate([a_n[c], rt_ref[rows[c], :]], axis=0),
                  jnp.concatenate([block(bt_ref[rows[c], :]), block(kt_ref[rows[c], :])], axis=0))
         for c in cs]
    a_ab = [jnp.where(strict_lower, m[c][:C, :W], 0.0) for c in cs]
    m_rb = [jnp.where(incl_lower, m[c][C:, :W], 0.0) for c in cs]
    av = [_bdot(jnp.concatenate([jnp.where(strict_lower, m[c][:C, W:], 0.0),
                                 jnp.where(incl_lower, m[c][C:, W:], 0.0)], axis=0),
                block(v_ref[rows[c], :])) for c in cs]
    t_inv = [eye_w + a_ab[c] for c in cs]
    p = a_ab
    for _ in range(5):
        p = [_bdot(p[c], block(p[c])) for c in cs]
        t_inv = [t_inv[c] + _bdot(t_inv[c], block(p[c])) for c in cs]
    x = [_bdot(t_inv[c], jnp.concatenate([a_s[c], block(av[c][:C])], axis=1)) for c in cs]
    ry = [_bdot(m_rb[c], jnp.concatenate([block(x[c][:, :W]), block(x[c][:, W:])], axis=1)) for c in cs]
    for c in cs:
        rp_ref[rows[c], :] = (rt_ref[rows[c], :] + ry[c][:, :W]).astype(BF16)
        yp_ref[rows[c], :] = av[c][C:] + ry[c][:, W:]
        bh_n = bh_ref[rows[c], :]
        p_bd = jnp.where(block_mask, _bdot_tn(bh_n, x[c][:, :W]), 0.0)
        g_bd = jnp.where(block_mask,
                         _bdot_tn(jnp.concatenate([bh_n, kh_ref[rows[c], :]], axis=0),
                                  jnp.concatenate([x[c][:, W:], v_ref[rows[c], :]], axis=0)), 0.0)
        pw_ref[rows[c], :] = fold(p_bd)
        gw_ref[rows[c], :] = fold(g_bd)
        gam_row = gam_ref[c * C:c * C + 1, :]
        gw2_ref[rows[c], :] = _split2_dot(eye_w * gam_row, seg)

    def chunk_map(c):
        return gw2_ref[rows[c], :], pw_ref[rows[c], :], gw_ref[rows[c], :], gam_ref[c * C:c * C + 1, :]

    def compose(first, second):
        d1, p1, g1, row1 = first
        d2, p2, g2, row2 = second
        cross = _bdot(p2, jnp.concatenate([block(p1), block(g1)], axis=1))
        return (d2 * d1, d2 * p1 + p2 * row1 + cross[:, :W], d2 * g1 + cross[:, W:] + g2, row2 * row1)

    def apply(mp, h, ph):
        return mp[0] * h + ph + mp[2]

    singles = [chunk_map(c) for c in cs]
    pairs = [compose(singles[2 * q], singles[2 * q + 1]) for q in range(n_chunks // 2)]
    quads = [compose(pairs[2 * q], pairs[2 * q + 1]) for q in range(n_chunks // 4)]
    starts = {0: h_ref[...]}
    for c in cs:
        spans = [(1, singles[c])]
        if c % 2 == 0:
            spans.append((2, pairs[c // 2]))
        if c % 4 == 0:
            spans.append((4, quads[c // 4]))
        spans = [(n, mp) for n, mp in spans if c + n not in starts]
        lhs = jnp.concatenate([rp_ref[rows[c], :]] + [mp[1].astype(BF16) for _, mp in spans], axis=0)
        out = _bdot(lhs, block(starts[c]))
        ys_ref[rows[c], :] = out[:C] + yp_ref[rows[c], :]
        for q, (n, mp) in enumerate(spans):
            starts[c + n] = apply(mp, starts[c], out[(q + 1) * C:(q + 2) * C])
    h_ref[...] = starts[n_chunks]

    y = ys_ref[...]
    inv = 1.0 / HEAD64
    mean = _bdot(y, seg) * inv
    d = y - mean
    var = _bdot(d * d, seg) * inv
    yn = d * lax.rsqrt(var + RWKV_LN_EPS) * lng_ref[...] + lnb_ref[...]
    bonus = _bdot(r * k2 * rk_ref[...], seg) * v
    y_ref[...] = ((yn + bonus) * g).astype(BF16)


def _rwkv(l, za, mu, w0, wd, a0, wa, wg, k_k, k_a, r_k, ln_g, ln_b, tm):
    B, S, _ = za.shape
    W = RWKV_WIDTH
    smalls = [mu, w0, wd, a0, wa, wg, k_k, k_a, r_k, ln_g, ln_b]
    return pl.pallas_call(
        _rwkv_kernel,
        grid=(B, S // tm),
        in_specs=[pl.BlockSpec((None, tm, 1024), lambda b, i: (b, i, 0))] + [_layer(s, l) for s in smalls],
        out_specs=pl.BlockSpec((None, tm, W), lambda b, i: (b, i, 0)),
        out_shape=jax.ShapeDtypeStruct((B, S, W), BF16),
        scratch_shapes=[pltpu.VMEM((8, 1024), F32), pltpu.VMEM((CHUNK, W), F32)]
                       + [pltpu.VMEM((tm, W), F32)] * 13 + [pltpu.VMEM((tm, W), BF16)],
        compiler_params=_params(("parallel", "arbitrary")),
        name="rwkv7_chunked",
    )(za, *smalls)


def _attn_kernel(q_ref, k_ref, vt_ref, o_ref, acc_ref, m_ref, s_ref, *, dv):
    qi = pl.program_id(1)
    H, tq, _ = q_ref.shape
    tk = vt_ref.shape[-1]
    acc_ref[...] = jnp.zeros_like(acc_ref)
    m_ref[...] = jnp.full_like(m_ref, MASK_VALUE)

    def step(j, nblk, q0, qn, masked):
        nk = nblk * tk
        rows = pl.ds(pl.multiple_of(j * tk, tk), nk)
        lanes = slice(q0, q0 + qn)
        if masked:
            key_le_query = _iota((nk, qn), 0) <= _iota((nk, qn), 1)
        m_news, alphas = [], []
        for hd in range(H):
            st = _bdot_nt(k_ref[hd, rows, :], q_ref[hd, lanes, :])
            if masked:
                st = jnp.where(key_le_query, st, MASK_VALUE)
            s_ref[hd, 0:nk, lanes] = st
            m_old = m_ref[hd, 0:1, lanes]
            m_new = jnp.maximum(m_old, jnp.max(st, axis=0, keepdims=True))
            m_ref[hd, 0:1, lanes] = m_new
            m_news.append(m_new)
            alphas.append(jnp.exp2(m_old - m_new))
        for hd in range(H):
            p = jnp.exp2(s_ref[hd, 0:nk, lanes] - m_news[hd]).astype(BF16)
            pv = sum(jnp.dot(vt_ref[hd, j + b], p[b * tk:(b + 1) * tk], preferred_element_type=F32)
                     for b in range(nblk))
            acc_ref[hd, :, lanes] = alphas[hd] * acc_ref[hd, :, lanes] + pv

    blocks_per_tile = tq // tk

    def body(jj, carry):
        step(jj * blocks_per_tile, blocks_per_tile, 0, tq, False)
        return carry

    lax.fori_loop(0, qi, body, 0)
    first_diag = qi * blocks_per_tile
    step(first_diag, 1, 0, tq, True)
    step(first_diag + 1, 1, tk, tq - tk, True)
    outs = []
    for hd in range(H):
        acc = acc_ref[hd]
        outs.append(acc[:dv] * (1.0 / acc[dv:dv + 1]))
    o_ref[...] = jnp.concatenate(outs, axis=0).T.astype(BF16)


def _attention(q, k, vt, dv, name):
    B, H, S, dq = q.shape
    _, _, _, dva, tk = vt.shape
    tq = 2 * tk
    return pl.pallas_call(
        functools.partial(_attn_kernel, dv=dv),
        grid=(B, S // tq),
        in_specs=[pl.BlockSpec((None, H, tq, dq), lambda b, i: (b, 0, i, 0)),
                  pl.BlockSpec((None, H, S, dq), lambda b, i: (b, 0, 0, 0)),
                  pl.BlockSpec((None, H, S // tk, dva, tk), lambda b, i: (b, 0, 0, 0, 0))],
        out_specs=pl.BlockSpec((None, tq, H * dv), lambda b, i: (b, i, 0)),
        out_shape=jax.ShapeDtypeStruct((B, S, H * dv), BF16),
        scratch_shapes=[pltpu.VMEM((H, dva, tq), F32), pltpu.VMEM((H, 8, tq), F32),
                        pltpu.VMEM((H, tq, tq), F32)],
        compiler_params=_params(("parallel", "arbitrary")),
        name=name,
    )(q, k, vt)


def _merge_kernel(x_ref, mod_ref, gpre_ref, gpost_ref, wg_ref, ya_ref, yb_ref, yc_ref,
                  wa_ref, wb_ref, wc_ref, wo_ref, o_ref):
    D = D_MODEL
    x = x_ref[...]
    h = (_rms(x, gpre_ref[...]) * (1.0 + mod_ref[1:2, :]) + mod_ref[0:1, :]).astype(BF16)
    merged = None
    for j, (y_ref, w_ref) in enumerate(((ya_ref, wa_ref), (yb_ref, wb_ref), (yc_ref, wc_ref))):
        gate = _sigmoid(jnp.dot(h, wg_ref[:, j * D:(j + 1) * D], preferred_element_type=F32))
        term = gate * jnp.dot(y_ref[...], w_ref[...], preferred_element_type=F32)
        merged = term if merged is None else merged + term
    out = _bdot(merged, wo_ref[...])
    o_ref[...] = x + mod_ref[2:3, :] * _rms(out, gpost_ref[...])


def _merge(l, x, mod, gpre, gpost, wg, ya, yb, yc, wa, wb, wc, wo, tm):
    B, S, D = x.shape
    tile = lambda w: pl.BlockSpec((None, tm, w), lambda b, i: (b, i, 0))
    return pl.pallas_call(
        _merge_kernel,
        grid=(B, S // tm),
        in_specs=[tile(D), pl.BlockSpec((None, None, 6, D), lambda b, i: (l, b, 0, 0)), _layer(gpre, l),
                  _layer(gpost, l), _layer(wg, l), tile(ya.shape[-1]), tile(yb.shape[-1]),
                  tile(yc.shape[-1]), _layer(wa, l), _layer(wb, l), _layer(wc, l),
                  _layer(wo, l)],
        out_specs=tile(D),
        out_shape=jax.ShapeDtypeStruct((B, S, D), F32),
        compiler_params=_params(("parallel", "parallel")),
        name="merge_out",
    )(x, mod, gpre, gpost, wg, ya, yb, yc, wa, wb, wc, wo)


def _ffn_kernel(x_ref, mod_ref, gpre_ref, gpost_ref, wu_ref, wd_ref, o_ref):
    D = D_MODEL
    x = x_ref[...]
    h = (_rms(x, gpre_ref[...]) * (1.0 + mod_ref[4:5, :]) + mod_ref[3:4, :]).astype(BF16)
    acc = None
    for j in range(D_FF // D):
        u = jnp.maximum(jnp.dot(h, wu_ref[:, j * D:(j + 1) * D], preferred_element_type=F32), 0.0)
        term = jnp.dot((u * u).astype(BF16), wd_ref[j * D:(j + 1) * D, :], preferred_element_type=F32)
        acc = term if acc is None else acc + term
    o_ref[...] = x + mod_ref[5:6, :] * _rms(acc, gpost_ref[...])


def _ffn(l, x, mod, gpre, gpost, wu, wd, tm):
    B, S, D = x.shape
    tile = pl.BlockSpec((None, tm, D), lambda b, i: (b, i, 0))
    return pl.pallas_call(
        _ffn_kernel,
        grid=(B, S // tm),
        in_specs=[tile, pl.BlockSpec((None, None, 6, D), lambda b, i: (l, b, 0, 0)), _layer(gpre, l),
                  _layer(gpost, l), _layer(wu, l), _layer(wd, l)],
        out_specs=tile,
        out_shape=jax.ShapeDtypeStruct((B, S, D), F32),
        compiler_params=_params(("parallel", "parallel")),
        name="ffn",
    )(x, mod, gpre, gpost, wu, wd)


def _inproj_weights_kernel(w_ref, w1_ref, wg_ref):
    w = w_ref[...].astype(F32)
    rb = w.shape[0]
    zeros = lambda n: jnp.zeros((rb, n), F32)
    w1 = jnp.concatenate([
        w[:, :_FOX0 + 768],
        w[:, _FOX0 + 768:_MLA0], zeros(128 - N_HEADS),
        w[:, _MLA0:_MLA0 + Q_LORA_RANK + KV_LORA_RANK],
        w[:, _GATE0 - QK_ROPE_DIM:_GATE0], zeros(128 - QK_ROPE_DIM),
    ], axis=1)
    w1_ref[...] = w1.astype(BF16)
    wg_ref[...] = w[:, _GATE0:_GATE0 + wg_ref.shape[-1]].astype(BF16)


def _inproj_weights(w_in, n_gate, rb=256):
    L, D, n_in = w_in.shape
    return pl.pallas_call(
        _inproj_weights_kernel,
        grid=(L, D // rb),
        in_specs=[pl.BlockSpec((None, rb, n_in), lambda l, i: (l, i, 0))],
        out_specs=[pl.BlockSpec((None, rb, _C_END), lambda l, i: (l, i, 0)),
                   pl.BlockSpec((None, rb, n_gate), lambda l, i: (l, i, 0))],
        out_shape=[jax.ShapeDtypeStruct((L, D, _C_END), BF16),
                   jax.ShapeDtypeStruct((L, D, n_gate), BF16)],
        compiler_params=_params(("parallel", "parallel")),
        name="inproj_weights",
    )(w_in)


def _pad_heads(w, dh, to):
    L, K, _ = w.shape
    return jnp.pad(w.reshape(L, K, N_HEADS, dh),
                   ((0, 0), (0, 0), (0, 0), (0, to - dh))).reshape(L, K, N_HEADS * to)


def _fox_selector():
    row = jnp.arange(384)[:, None]
    col = jnp.arange(512)[None, :]
    part, hd = row // 128, row % 128
    valid = hd < N_HEADS
    sel = (jnp.where(valid & (col == hd * 128 + 64 + part), 1.0, 0.0)
           - jnp.where(valid & (col == hd * 128 + 67 + part), 1.0, 0.0))
    return sel.astype(BF16)


def _rows(v):
    return v.reshape(v.shape[0], 1, -1)


def kernel(x, c, positions, w_in, mu_shift, w0, w_decay_up, a0, w_aaa_up, w_gate_up, k_k, k_a, r_k,
           ln_x_g, ln_x_b, b_forget, q_norm_g, w_q_up, kv_norm_g, w_kv_up, w_branch_a, w_branch_b,
           w_branch_c, w_out, w_mod, b_mod, norm_mix_pre, norm_mix_post, norm_ffn_pre, norm_ffn_post,
           w_ffn_up, w_ffn_down):
    B, S, D = x.shape
    L = w_in.shape[0]
    tm = min(512, S)
    bf16 = lambda w: w.astype(BF16)

    c8 = jnp.pad(c, ((0, 8 - B), (0, 0)))
    mod = _modulation(c8, w_mod, b_mod)[:, :B].reshape(L, B, 6, D)
    cos_t, sin_t = _rope_tables(positions, tm)
    sel = _fox_selector()

    mq_scale = (QK_NOPE_DIM + QK_ROPE_DIM) ** -0.5 * LOG2E
    n_in = w_in.shape[-1]
    w_in_padded = bf16(jnp.pad(w_in, ((0, 0), (0, 0), (0, -n_in % 128))))
    w1, w_gates = _inproj_weights(w_in_padded, n_in - _GATE0)
    bf = jnp.pad(_rows(b_forget), ((0, 0), (0, 0), (0, 128 - N_HEADS)))
    wq = _pad_heads(bf16(w_q_up * mq_scale), QK_NOPE_DIM + QK_ROPE_DIM, 256)
    wkv = bf16(w_kv_up)
    zeros64 = jnp.zeros((L, 64, RWKV_WIDTH), F32)
    wd = jnp.concatenate([w_decay_up, zeros64], axis=1)
    wa = jnp.concatenate([zeros64, w_aaa_up], axis=1)
    wba, wbb, wbc, wo = bf16(w_branch_a), bf16(w_branch_b), bf16(w_branch_c), bf16(w_out)
    wu, wdn = bf16(w_ffn_up), bf16(w_ffn_down)
    g_mix_pre, g_mix_post = _rows(norm_mix_pre), _rows(norm_mix_post)
    g_ffn_pre, g_ffn_post = _rows(norm_ffn_pre), _rows(norm_ffn_post)
    rwkv_rows = [_rows(v) for v in (mu_shift, w0)] + [wd, _rows(a0), wa, w_gate_up] + [
        _rows(v) for v in (k_k, k_a, r_k.reshape(L, -1), ln_x_g, ln_x_b)]
    qg, kvg = _rows(q_norm_g), _rows(kv_norm_g)

    for l in range(L):
        za, fq, fk, fv, mq, mk, mv = _inproj(l, x, mod, g_mix_pre, w1, bf, sel, qg, wq,
                                             kvg, wkv, cos_t, sin_t, tm)
        ya = _rwkv(l, za, *rwkv_rows, min(2 * tm, S))
        yb = _attention(fq, fk, fv, HEAD64, "fox_attention")
        yc = _attention(mq, mk, mv, V_HEAD_DIM, "mla_attention")
        x = _merge(l, x, mod, g_mix_pre, g_mix_post, w_gates, ya, yb, yc, wba, wbb, wbc, wo, tm)
        x = _ffn(l, x, mod, g_ffn_pre, g_ffn_post, wu, wdn, tm)
    return x
```

```python
import functools

import jax
import jax.numpy as jnp
from jax import lax
from jax.experimental import pallas as pl
from jax.experimental.pallas import tpu as pltpu

F32 = jnp.float32
BF16 = jnp.bfloat16

D_MODEL = 1024
N_HEADS = 4
HEAD64 = 64
RWKV_WIDTH = N_HEADS * HEAD64
QK_NOPE_DIM = 128
QK_ROPE_DIM = 64
V_HEAD_DIM = 128
Q_LORA_RANK = 384
KV_LORA_RANK = 256
ROPE_THETA = 10000.0
D_FF = 4 * D_MODEL
NORM_EPS = 1e-6
RWKV_LN_EPS = 64e-5
MASK_VALUE = -1e30
LOG2E = 1.4426950408889634
EXP_NEG_HALF = 0.6065306597126334
FOX_Q_SCALE = HEAD64 ** -0.5 * LOG2E
CHUNK = 64

_FOX0 = 1024
_MLA0 = _FOX0 + 3 * 256 + N_HEADS
_GATE0 = _MLA0 + Q_LORA_RANK + KV_LORA_RANK + QK_ROPE_DIM

_C_RWKV = 0
_C_FQ = 1024
_C_FK = 1280
_C_FV = 1536
_C_FF = 1792
_C_QL = 1920
_C_KVL = 2304
_C_KPE = 2560
_C_END = 2688

VMEM_LIMIT = 56 * 1024 * 1024


def _bdot(a, b):
    return jnp.dot(a.astype(BF16), b.astype(BF16), preferred_element_type=F32)


def _bdot_nt(a, b):
    return lax.dot_general(a.astype(BF16), b.astype(BF16), (((1,), (1,)), ((), ())),
                           preferred_element_type=F32)


def _bdot_tn(a, b):
    return lax.dot_general(a.astype(BF16), b.astype(BF16), (((0,), (0,)), ((), ())),
                           preferred_element_type=F32)


def _split2_dot(a, b_bf16):
    hi = a.astype(BF16)
    lo = (a - hi.astype(F32)).astype(BF16)
    return (jnp.dot(hi, b_bf16, preferred_element_type=F32)
            + jnp.dot(lo, b_bf16, preferred_element_type=F32))


def _split3(a):
    hi = a.astype(BF16)
    r1 = a - hi.astype(F32)
    mid = r1.astype(BF16)
    lo = (r1 - mid.astype(F32)).astype(BF16)
    return hi, mid, lo


def _rms(x, gain):
    return x * lax.rsqrt(jnp.mean(x * x, axis=-1, keepdims=True) + NORM_EPS) * gain


def _softplus(y):
    return jnp.maximum(y, 0.0) + jnp.log(1.0 + jnp.exp(-jnp.abs(y)))


def _sigmoid(y):
    return 1.0 / (1.0 + jnp.exp(-y))


def _iota(shape, dim):
    return lax.broadcasted_iota(jnp.int32, shape, dim)


def _full(shape):
    return pl.BlockSpec(shape, lambda *_: (0,) * len(shape))


def _layer(arr, l, single_buffer=False):
    rest = arr.shape[1:]
    mode = pl.Buffered(1) if single_buffer else None
    return pl.BlockSpec((None,) + rest, lambda *_: (l,) + (0,) * len(rest), pipeline_mode=mode)


def _params(sem):
    return pltpu.CompilerParams(dimension_semantics=sem, vmem_limit_bytes=VMEM_LIMIT)


def _mod_kernel(c_ref, w_ref, b_ref, o_ref):
    c = c_ref[...]
    o_ref[...] = _split2_dot(c * _sigmoid(c), w_ref[...].astype(BF16)) + b_ref[...]


def _modulation(c8, w_mod, b_mod):
    L, D, D6 = w_mod.shape
    return pl.pallas_call(
        _mod_kernel,
        grid=(L, D6 // D),
        in_specs=[pl.BlockSpec((8, D), lambda l, j: (0, 0)),
                  pl.BlockSpec((None, D, D), lambda l, j: (l, 0, j)),
                  pl.BlockSpec((None, 1, D), lambda l, j: (l, 0, j))],
        out_specs=pl.BlockSpec((None, 8, D), lambda l, j: (l, 0, j)),
        out_shape=jax.ShapeDtypeStruct((L, 8, D6), F32),
        compiler_params=_params(("parallel", "parallel")),
        name="adaln_mod",
    )(c8, w_mod, b_mod.reshape(L, 1, D6))


def _rope_kernel(pos_ref, inv_ref, cos_ref, sin_ref):
    ang = pos_ref[...] * inv_ref[...]
    lane = _iota(ang.shape, 1)
    cos_ref[...] = jnp.where(lane < 64, jnp.cos(ang), 0.0)
    s = jnp.sin(ang)
    sin_ref[...] = jnp.where(lane < 32, -s, jnp.where(lane < 64, s, 0.0))


def _rope_tables(positions, tm):
    B, S = positions.shape
    inv_freq = ROPE_THETA ** (-jnp.arange(0, QK_ROPE_DIM, 2, dtype=F32) / QK_ROPE_DIM)
    inv128 = jnp.tile(inv_freq, 4).reshape(1, 128)
    pos = positions.astype(F32).reshape(B, S, 1)
    out = jax.ShapeDtypeStruct((B, S, 128), F32)
    return pl.pallas_call(
        _rope_kernel,
        grid=(B, S // tm),
        in_specs=[pl.BlockSpec((None, tm, 1), lambda b, i: (b, i, 0)), _full((1, 128))],
        out_specs=[pl.BlockSpec((None, tm, 128), lambda b, i: (b, i, 0))] * 2,
        out_shape=[out, out],
        compiler_params=_params(("parallel", "parallel")),
        name="rope_tables",
    )(pos, inv128)


def _rope128(x, cos_t, sin_t):
    lane = _iota(x.shape, 1)
    partner = jnp.where(lane < 32, pltpu.roll(x, 96, 1), pltpu.roll(x, 32, 1))
    return x * cos_t + partner * sin_t


def _inproj_kernel(x_ref, mod_ref, gpre_ref, w1_ref, bf_ref, sel_ref,
                   qg_ref, wq_ref, kvg_ref, wkv_ref, cos_ref, sin_ref,
                   za_ref, fq_ref, fk_ref, fv_ref, mq_ref, mk_ref, mv_ref, carry_ref):
    i = pl.program_id(1)
    tm = x_ref.shape[0]

    @pl.when(i == 0)
    def _():
        carry_ref[...] = jnp.zeros_like(carry_ref)

    x = x_ref[...]
    h = (_rms(x, gpre_ref[...]) * (1.0 + mod_ref[1:2, :]) + mod_ref[0:1, :]).astype(BF16)

    za_ref[...] = jnp.dot(h, w1_ref[:, _C_RWKV:_C_FQ], preferred_element_type=F32)

    f = jnp.dot(h, w1_ref[:, _C_FF:_C_QL], preferred_element_type=F32) + bf_ref[...]
    logf = -_softplus(-f)
    tri = (_iota((tm, tm), 0) >= _iota((tm, tm), 1)).astype(BF16)
    l_hi, l_mid, l_lo = _split3(logf)
    cum = (jnp.dot(tri, l_hi, preferred_element_type=F32)
           + jnp.dot(tri, l_mid, preferred_element_type=F32)
           + jnp.dot(tri, l_lo, preferred_element_type=F32)) + carry_ref[0:1, :]
    carry_ref[0:1, :] = cum[tm - 1:tm, :]
    cc = jnp.concatenate(_split3(cum * LOG2E), axis=1)
    aug = jnp.dot(cc, sel_ref[...], preferred_element_type=F32)
    fqk = jnp.dot(h, w1_ref[:, _C_FQ:_C_FV], preferred_element_type=F32)
    lane = _iota((1, 128), 1)
    q_ones = (lane >= 67) & (lane < 70)
    k_ones = (lane >= 64) & (lane < 67)
    fv_t = jnp.dot(h, w1_ref[:, _C_FV:_C_FF], preferred_element_type=F32).T
    ones_rows = (_iota((16, tm), 0) == 0).astype(BF16)
    for hd in range(N_HEADS):
        aug_h = aug[:, 128 * hd:128 * hd + 128]
        q_grp = fqk[:, 128 * (hd // 2):128 * (hd // 2) + 128] * FOX_Q_SCALE
        k_grp = fqk[:, 256 + 128 * (hd // 2):256 + 128 * (hd // 2) + 128]
        if hd % 2:
            q_grp = pltpu.roll(q_grp, 64, 1)
            k_grp = pltpu.roll(k_grp, 64, 1)
        fq_ref[hd] = jnp.where(lane < 64, q_grp, jnp.where(q_ones, 1.0, aug_h)).astype(BF16)
        fk_ref[hd] = jnp.where(lane < 64, k_grp, jnp.where(k_ones, 1.0, aug_h)).astype(BF16)
        fv_ref[hd] = jnp.concatenate([fv_t[64 * hd:64 * hd + 64].astype(BF16), ones_rows], axis=0)

    cos_t = cos_ref[...]
    sin_t = sin_ref[...]
    q_lat = jnp.dot(h, w1_ref[:, _C_QL:_C_KVL], preferred_element_type=F32)
    q = _bdot(_rms(q_lat, qg_ref[...]), wq_ref[...])
    kv_lat = jnp.dot(h, w1_ref[:, _C_KVL:_C_KPE], preferred_element_type=F32)
    kv = _bdot(_rms(kv_lat, kvg_ref[...]), wkv_ref[...])
    k_pe = _rope128(jnp.dot(h, w1_ref[:, _C_KPE:_C_END], preferred_element_type=F32), cos_t, sin_t)
    k_pe = k_pe.astype(BF16)
    for hd in range(N_HEADS):
        o = 256 * hd
        q_pe = _rope128(q[:, o + 128:o + 256], cos_t, sin_t)
        mq_ref[hd] = jnp.concatenate([q[:, o:o + 128], q_pe], axis=1).astype(BF16)
        mk_ref[hd] = jnp.concatenate([kv[:, o:o + 128].astype(BF16), k_pe], axis=1)
        mv_ref[hd] = jnp.concatenate([kv[:, o + 128:o + 256].T.astype(BF16), ones_rows], axis=0)


def _inproj(l, x, mod, gpre, w1, bf, sel, qg, wq, kvg, wkv, cos_t, sin_t, tm):
    B, S, D = x.shape
    H = N_HEADS
    tile = lambda w: pl.BlockSpec((None, tm, w), lambda b, i: (b, i, 0))
    head = lambda w: pl.BlockSpec((None, H, tm, w), lambda b, i: (b, 0, i, 0))
    hs = lambda w: jax.ShapeDtypeStruct((B, H, S, w), BF16)
    head_t = lambda r: pl.BlockSpec((None, H, None, r, tm), lambda b, i: (b, 0, i, 0, 0))
    hs_t = lambda r: jax.ShapeDtypeStruct((B, H, S // tm, r, tm), BF16)
    return pl.pallas_call(
        _inproj_kernel,
        grid=(B, S // tm),
        in_specs=[tile(D), pl.BlockSpec((None, None, 6, D), lambda b, i: (l, b, 0, 0)), _layer(gpre, l),
                  _layer(w1, l), _layer(bf, l), _full(sel.shape), _layer(qg, l), _layer(wq, l),
                  _layer(kvg, l), _layer(wkv, l), tile(128), tile(128)],
        out_specs=[tile(1024), head(128), head(128), head_t(80), head(256), head(256), head_t(144)],
        out_shape=[jax.ShapeDtypeStruct((B, S, 1024), F32), hs(128), hs(128), hs_t(80),
                   hs(256), hs(256), hs_t(144)],
        scratch_shapes=[pltpu.VMEM((8, 128), F32)],
        compiler_params=_params(("parallel", "arbitrary")),
        name="in_proj",
    )(x, mod, gpre, w1, bf, sel, qg, wq, kvg, wkv, cos_t, sin_t)


def _rwkv_kernel(z_ref, mu_ref, w0_ref, wd_ref, a0_ref, wa_ref, wg_ref, kk_ref, ka_ref, rk_ref,
                 lng_ref, lnb_ref, y_ref,
                 prev_ref, h_ref, rt_ref, at_ref, bt_ref, kt_ref, bh_ref, kh_ref, v_ref, gam_ref,
                 ys_ref, yp_ref, gw_ref, gw2_ref, pw_ref, rp_ref):
    i = pl.program_id(1)
    tm = z_ref.shape[0]
    W = RWKV_WIDTH
    C = CHUNK

    @pl.when(i == 0)
    def _():
        prev_ref[...] = jnp.zeros_like(prev_ref)
        h_ref[...] = jnp.zeros_like(h_ref)

    z = z_ref[...]
    zroll = pltpu.roll(z, 1, 0)
    zprev = jnp.concatenate([jnp.where(_iota((8, 1), 0) == 0, prev_ref[0:1, :], zroll[0:8]), zroll[8:]], axis=0)
    prev_ref[0:1, :] = z[tm - 1:tm, :]
    zm = z + (zprev - z) * mu_ref[...]
    r = zm[:, 0:W]
    k = zm[:, W:2 * W]
    v = zm[:, 2 * W:3 * W]
    lora_in = zm[:, 3 * W:3 * W + 128]
    gd = zm[:, 3 * W + 128:3 * W + 256]
    lw = -EXP_NEG_HALF * _sigmoid(w0_ref[...] + _bdot(jnp.tanh(lora_in), wd_ref[...]))
    a = _sigmoid(a0_ref[...] + _bdot(lora_in, wa_ref[...]))
    g = _bdot(_sigmoid(gd), wg_ref[...])
    seg = ((_iota((W, W), 0) // HEAD64) == (_iota((W, W), 1) // HEAD64)).astype(BF16)
    kk = k * kk_ref[...]
    kk = kk * lax.rsqrt(jnp.maximum(_bdot(kk * kk, seg), 1e-24))
    k2 = k * (1.0 + (a - 1.0) * ka_ref[...])

    tri = (_iota((C, C), 0) >= _iota((C, C), 1)).astype(BF16)
    lw3 = _split3(lw)
    cls, ces = [], []
    for c in range(tm // C):
        cl_c = sum(jnp.dot(tri, part[c * C:(c + 1) * C, :], preferred_element_type=F32) for part in lw3)
        cls.append(cl_c)
        ces.append(jnp.broadcast_to(cl_c[C - 1:C, :], (C, W)))
    cl = jnp.concatenate(cls, axis=0)
    ce = jnp.concatenate(ces, axis=0)
    e_neg = jnp.exp(-cl)
    e_end = jnp.exp(ce - cl)
    rt_ref[...] = r * jnp.exp(cl)
    at_ref[...] = -kk * jnp.exp(cl - lw)
    bt_ref[...] = kk * a * e_neg
    kt_ref[...] = k2 * e_neg
    bh_ref[...] = kk * a * e_end
    kh_ref[...] = k2 * e_end
    v_ref[...] = v
    gam_ref[...] = jnp.exp(ce)

    block_mask = (_iota((W, W), 0) // C) == (_iota((W, W), 1) // C)
    t_row = _iota((C, W), 0)
    s_col = _iota((C, W), 1) % C
    strict_lower = t_row > s_col
    incl_lower = t_row >= s_col
    eye_w = (t_row == s_col).astype(F32)

    def block(x):
        return jnp.where(block_mask, jnp.concatenate([x] * N_HEADS, axis=0), 0.0).astype(BF16)

    def fold(x_bd):
        return x_bd[0:C] + x_bd[C:2 * C] + x_bd[2 * C:3 * C] + x_bd[3 * C:4 * C]

    n_chunks = tm // C
    cs = range(n_chunks)
    rows = [slice(c * C, (c + 1) * C) for c in cs]
    a_n = [at_ref[rows[c], :] for c in cs]
    a_s = [block(a_n[c]) for c in cs]
    m = [_bdot_nt(jnp.concatenate([a_n[c], rt_ref[rows[c], :]], axis=0),
                  jnp.concatenate([block(bt_ref[rows[c], :]), block(kt_ref[rows[c], :])], axis=0))
         for c in cs]
    a_ab = [jnp.where(strict_lower, m[c][:C, :W], 0.0) for c in cs]
    m_rb = [jnp.where(incl_lower, m[c][C:, :W], 0.0) for c in cs]
    av = [_bdot(jnp.concatenate([jnp.where(strict_lower, m[c][:C, W:], 0.0),
                                 jnp.where(incl_lower, m[c][C:, W:], 0.0)], axis=0),
                block(v_ref[rows[c], :])) for c in cs]
    t_inv = [eye_w + a_ab[c] for c in cs]
    p = a_ab
    for _ in range(5):
        p = [_bdot(p[c], block(p[c])) for c in cs]
        t_inv = [t_inv[c] + _bdot(t_inv[c], block(p[c])) for c in cs]
    x = [_bdot(t_inv[c], jnp.concatenate([a_s[c], block(av[c][:C])], axis=1)) for c in cs]
    ry = [_bdot(m_rb[c], jnp.concatenate([block(x[c][:, :W]), block(x[c][:, W:])], axis=1)) for c in cs]
    for c in cs:
        rp_ref[rows[c], :] = (rt_ref[rows[c], :] + ry[c][:, :W]).astype(BF16)
        yp_ref[rows[c], :] = av[c][C:] + ry[c][:, W:]
        bh_n = bh_ref[rows[c], :]
        p_bd = jnp.where(block_mask, _bdot_tn(bh_n, x[c][:, :W]), 0.0)
        g_bd = jnp.where(block_mask,
                         _bdot_tn(jnp.concatenate([bh_n, kh_ref[rows[c], :]], axis=0),
                                  jnp.concatenate([x[c][:, W:], v_ref[rows[c], :]], axis=0)), 0.0)
        pw_ref[rows[c], :] = fold(p_bd)
        gw_ref[rows[c], :] = fold(g_bd)
        gam_row = gam_ref[c * C:c * C + 1, :]
        gw2_ref[rows[c], :] = _split2_dot(eye_w * gam_row, seg)

    def chunk_map(c):
        return gw2_ref[rows[c], :], pw_ref[rows[c], :], gw_ref[rows[c], :], gam_ref[c * C:c * C + 1, :]

    def compose(first, second):
        d1, p1, g1, row1 = first
        d2, p2, g2, row2 = second
        cross = _bdot(p2, jnp.concatenate([block(p1), block(g1)], axis=1))
        return (d2 * d1, d2 * p1 + p2 * row1 + cross[:, :W], d2 * g1 + cross[:, W:] + g2, row2 * row1)

    def apply(mp, h, ph):
        return mp[0] * h + ph + mp[2]

    singles = [chunk_map(c) for c in cs]
    pairs = [compose(singles[2 * q], singles[2 * q + 1]) for q in range(n_chunks // 2)]
    quads = [compose(pairs[2 * q], pairs[2 * q + 1]) for q in range(n_chunks // 4)]
    starts = {0: h_ref[...]}
    for c in cs:
        spans = [(1, singles[c])]
        if c % 2 == 0:
            spans.append((2, pairs[c // 2]))
        if c % 4 == 0:
            spans.append((4, quads[c // 4]))
        spans = [(n, mp) for n, mp in spans if c + n not in starts]
        lhs = jnp.concatenate([rp_ref[rows[c], :]] + [mp[1].astype(BF16) for _, mp in spans], axis=0)
        out = _bdot(lhs, block(starts[c]))
        ys_ref[rows[c], :] = out[:C] + yp_ref[rows[c], :]
        for q, (n, mp) in enumerate(spans):
            starts[c + n] = apply(mp, starts[c], out[(q + 1) * C:(q + 2) * C])
    h_ref[...] = starts[n_chunks]

    y = ys_ref[...]
    inv = 1.0 / HEAD64
    mean = _bdot(y, seg) * inv
    d = y - mean
    var = _bdot(d * d, seg) * inv
    yn = d * lax.rsqrt(var + RWKV_LN_EPS) * lng_ref[...] + lnb_ref[...]
    bonus = _bdot(r * k2 * rk_ref[...], seg) * v
    y_ref[...] = ((yn + bonus) * g).astype(BF16)


def _rwkv(l, za, mu, w0, wd, a0, wa, wg, k_k, k_a, r_k, ln_g, ln_b, tm):
    B, S, _ = za.shape
    W = RWKV_WIDTH
    smalls = [mu, w0, wd, a0, wa, wg, k_k, k_a, r_k, ln_g, ln_b]
    return pl.pallas_call(
        _rwkv_kernel,
        grid=(B, S // tm),
        in_specs=[pl.BlockSpec((None, tm, 1024), lambda b, i: (b, i, 0))] + [_layer(s, l) for s in smalls],
        out_specs=pl.BlockSpec((None, tm, W), lambda b, i: (b, i, 0)),
        out_shape=jax.ShapeDtypeStruct((B, S, W), BF16),
        scratch_shapes=[pltpu.VMEM((8, 1024), F32), pltpu.VMEM((CHUNK, W), F32)]
                       + [pltpu.VMEM((tm, W), F32)] * 13 + [pltpu.VMEM((tm, W), BF16)],
        compiler_params=_params(("parallel", "arbitrary")),
        name="rwkv7_chunked",
    )(za, *smalls)


def _attn_kernel(q_ref, k_ref, vt_ref, o_ref, acc_ref, m_ref, s_ref, *, dv):
    qi = pl.program_id(1)
    H, tq, _ = q_ref.shape
    tk = vt_ref.shape[-1]
    acc_ref[...] = jnp.zeros_like(acc_ref)
    m_ref[...] = jnp.full_like(m_ref, MASK_VALUE)

    def step(j, nblk, q0, qn, masked):
        nk = nblk * tk
        rows = pl.ds(pl.multiple_of(j * tk, tk), nk)
        lanes = slice(q0, q0 + qn)
        if masked:
            key_le_query = _iota((nk, qn), 0) <= _iota((nk, qn), 1)
        m_news, alphas = [], []
        for hd in range(H):
            st = _bdot_nt(k_ref[hd, rows, :], q_ref[hd, lanes, :])
            if masked:
                st = jnp.where(key_le_query, st, MASK_VALUE)
            s_ref[hd, 0:nk, lanes] = st
            m_old = m_ref[hd, 0:1, lanes]
            m_new = jnp.maximum(m_old, jnp.max(st, axis=0, keepdims=True))
            m_ref[hd, 0:1, lanes] = m_new
            m_news.append(m_new)
            alphas.append(jnp.exp2(m_old - m_new))
        for hd in range(H):
            p = jnp.exp2(s_ref[hd, 0:nk, lanes] - m_news[hd]).astype(BF16)
            pv = sum(jnp.dot(vt_ref[hd, j + b], p[b * tk:(b + 1) * tk], preferred_element_type=F32)
                     for b in range(nblk))
            acc_ref[hd, :, lanes] = alphas[hd] * acc_ref[hd, :, lanes] + pv

    blocks_per_tile = tq // tk

    def body(jj, carry):
        step(jj * blocks_per_tile, blocks_per_tile, 0, tq, False)
        return carry

    lax.fori_loop(0, qi, body, 0)
    first_diag = qi * blocks_per_tile
    step(first_diag, 1, 0, tq, True)
    step(first_diag + 1, 1, tk, tq - tk, True)
    outs = []
    for hd in range(H):
        acc = acc_ref[hd]
        outs.append(acc[:dv] * (1.0 / acc[dv:dv + 1]))
    o_ref[...] = jnp.concatenate(outs, axis=0).T.astype(BF16)


def _attention(q, k, vt, dv, name):
    B, H, S, dq = q.shape
    _, _, _, dva, tk = vt.shape
    tq = 2 * tk
    return pl.pallas_call(
        functools.partial(_attn_kernel, dv=dv),
        grid=(B, S // tq),
        in_specs=[pl.BlockSpec((None, H, tq, dq), lambda b, i: (b, 0, i, 0)),
                  pl.BlockSpec((None, H, S, dq), lambda b, i: (b, 0, 0, 0)),
                  pl.BlockSpec((None, H, S // tk, dva, tk), lambda b, i: (b, 0, 0, 0, 0))],
        out_specs=pl.BlockSpec((None, tq, H * dv), lambda b, i: (b, i, 0)),
        out_shape=jax.ShapeDtypeStruct((B, S, H * dv), BF16),
        scratch_shapes=[pltpu.VMEM((H, dva, tq), F32), pltpu.VMEM((H, 8, tq), F32),
                        pltpu.VMEM((H, tq, tq), F32)],
        compiler_params=_params(("parallel", "arbitrary")),
        name=name,
    )(q, k, vt)


def _merge_kernel(x_ref, mod_ref, gpre_ref, gpost_ref, wg_ref, ya_ref, yb_ref, yc_ref,
                  wa_ref, wb_ref, wc_ref, wo_ref, o_ref):
    D = D_MODEL
    x = x_ref[...]
    h = (_rms(x, gpre_ref[...]) * (1.0 + mod_ref[1:2, :]) + mod_ref[0:1, :]).astype(BF16)
    merged = None
    for j, (y_ref, w_ref) in enumerate(((ya_ref, wa_ref), (yb_ref, wb_ref), (yc_ref, wc_ref))):
        gate = _sigmoid(jnp.dot(h, wg_ref[:, j * D:(j + 1) * D], preferred_element_type=F32))
        term = gate * jnp.dot(y_ref[...], w_ref[...], preferred_element_type=F32)
        merged = term if merged is None else merged + term
    out = _bdot(merged, wo_ref[...])
    o_ref[...] = x + mod_ref[2:3, :] * _rms(out, gpost_ref[...])


def _merge(l, x, mod, gpre, gpost, wg, ya, yb, yc, wa, wb, wc, wo, tm):
    B, S, D = x.shape
    tile = lambda w: pl.BlockSpec((None, tm, w), lambda b, i: (b, i, 0))
    return pl.pallas_call(
        _merge_kernel,
        grid=(B, S // tm),
        in_specs=[tile(D), pl.BlockSpec((None, None, 6, D), lambda b, i: (l, b, 0, 0)), _layer(gpre, l),
                  _layer(gpost, l), _layer(wg, l), tile(ya.shape[-1]), tile(yb.shape[-1]),
                  tile(yc.shape[-1]), _layer(wa, l), _layer(wb, l), _layer(wc, l),
                  _layer(wo, l)],
        out_specs=tile(D),
        out_shape=jax.ShapeDtypeStruct((B, S, D), F32),
        compiler_params=_params(("parallel", "parallel")),
        name="merge_out",
    )(x, mod, gpre, gpost, wg, ya, yb, yc, wa, wb, wc, wo)


def _ffn_kernel(x_ref, mod_ref, gpre_ref, gpost_ref, wu_ref, wd_ref, o_ref):
    D = D_MODEL
    x = x_ref[...]
    h = (_rms(x, gpre_ref[...]) * (1.0 + mod_ref[4:5, :]) + mod_ref[3:4, :]).astype(BF16)
    acc = None
    for j in range(D_FF // D):
        u = jnp.maximum(jnp.dot(h, wu_ref[:, j * D:(j + 1) * D], preferred_element_type=F32), 0.0)
        term = jnp.dot((u * u).astype(BF16), wd_ref[j * D:(j + 1) * D, :], preferred_element_type=F32)
        acc = term if acc is None else acc + term
    o_ref[...] = x + mod_ref[5:6, :] * _rms(acc, gpost_ref[...])


def _ffn(l, x, mod, gpre, gpost, wu, wd, tm):
    B, S, D = x.shape
    tile = pl.BlockSpec((None, tm, D), lambda b, i: (b, i, 0))
    return pl.pallas_call(
        _ffn_kernel,
        grid=(B, S // tm),
        in_specs=[tile, pl.BlockSpec((None, None, 6, D), lambda b, i: (l, b, 0, 0)), _layer(gpre, l),
                  _layer(gpost, l), _layer(wu, l, True), _layer(wd, l, True)],
        out_specs=tile,
        out_shape=jax.ShapeDtypeStruct((B, S, D), F32),
        compiler_params=_params(("parallel", "parallel")),
        name="ffn",
    )(x, mod, gpre, gpost, wu, wd)


def _inproj_weights_kernel(w_ref, w1_ref, wg_ref):
    w = w_ref[...].astype(F32)
    rb = w.shape[0]
    zeros = lambda n: jnp.zeros((rb, n), F32)
    w1 = jnp.concatenate([
        w[:, :_FOX0 + 768],
        w[:, _FOX0 + 768:_MLA0], zeros(128 - N_HEADS),
        w[:, _MLA0:_MLA0 + Q_LORA_RANK + KV_LORA_RANK],
        w[:, _GATE0 - QK_ROPE_DIM:_GATE0], zeros(128 - QK_ROPE_DIM),
    ], axis=1)
    w1_ref[...] = w1.astype(BF16)
    wg_ref[...] = w[:, _GATE0:_GATE0 + wg_ref.shape[-1]].astype(BF16)


def _inproj_weights(w_in, n_gate, rb=256):
    L, D, n_in = w_in.shape
    return pl.pallas_call(
        _inproj_weights_kernel,
        grid=(L, D // rb),
        in_specs=[pl.BlockSpec((None, rb, n_in), lambda l, i: (l, i, 0))],
        out_specs=[pl.BlockSpec((None, rb, _C_END), lambda l, i: (l, i, 0)),
                   pl.BlockSpec((None, rb, n_gate), lambda l, i: (l, i, 0))],
        out_shape=[jax.ShapeDtypeStruct((L, D, _C_END), BF16),
                   jax.ShapeDtypeStruct((L, D, n_gate), BF16)],
        compiler_params=_params(("parallel", "parallel")),
        name="inproj_weights",
    )(w_in)


def _pad_heads(w, dh, to):
    L, K, _ = w.shape
    return jnp.pad(w.reshape(L, K, N_HEADS, dh),
                   ((0, 0), (0, 0), (0, 0), (0, to - dh))).reshape(L, K, N_HEADS * to)


def _fox_selector():
    row = jnp.arange(384)[:, None]
    col = jnp.arange(512)[None, :]
    part, hd = row // 128, row % 128
    valid = hd < N_HEADS
    sel = (jnp.where(valid & (col == hd * 128 + 64 + part), 1.0, 0.0)
           - jnp.where(valid & (col == hd * 128 + 67 + part), 1.0, 0.0))
    return sel.astype(BF16)


def _rows(v):
    return v.reshape(v.shape[0], 1, -1)


def kernel(x, c, positions, w_in, mu_shift, w0, w_decay_up, a0, w_aaa_up, w_gate_up, k_k, k_a, r_k,
           ln_x_g, ln_x_b, b_forget, q_norm_g, w_q_up, kv_norm_g, w_kv_up, w_branch_a, w_branch_b,
           w_branch_c, w_out, w_mod, b_mod, norm_mix_pre, norm_mix_post, norm_ffn_pre, norm_ffn_post,
           w_ffn_up, w_ffn_down):
    B, S, D = x.shape
    L = w_in.shape[0]
    tm = min(512, S)
    bf16 = lambda w: w.astype(BF16)

    c8 = jnp.pad(c, ((0, 8 - B), (0, 0)))
    mod = _modulation(c8, w_mod, b_mod)[:, :B].reshape(L, B, 6, D)
    cos_t, sin_t = _rope_tables(positions, tm)
    sel = _fox_selector()

    mq_scale = (QK_NOPE_DIM + QK_ROPE_DIM) ** -0.5 * LOG2E
    n_in = w_in.shape[-1]
    w_in_padded = bf16(jnp.pad(w_in, ((0, 0), (0, 0), (0, -n_in % 128))))
    w1, w_gates = _inproj_weights(w_in_padded, n_in - _GATE0)
    bf = jnp.pad(_rows(b_forget), ((0, 0), (0, 0), (0, 128 - N_HEADS)))
    wq = _pad_heads(bf16(w_q_up * mq_scale), QK_NOPE_DIM + QK_ROPE_DIM, 256)
    wkv = bf16(w_kv_up)
    zeros64 = jnp.zeros((L, 64, RWKV_WIDTH), F32)
    wd = jnp.concatenate([w_decay_up, zeros64], axis=1)
    wa = jnp.concatenate([zeros64, w_aaa_up], axis=1)
    wba, wbb, wbc, wo = bf16(w_branch_a), bf16(w_branch_b), bf16(w_branch_c), bf16(w_out)
    wu, wdn = bf16(w_ffn_up), bf16(w_ffn_down)
    g_mix_pre, g_mix_post = _rows(norm_mix_pre), _rows(norm_mix_post)
    g_ffn_pre, g_ffn_post = _rows(norm_ffn_pre), _rows(norm_ffn_post)
    rwkv_rows = [_rows(v) for v in (mu_shift, w0)] + [wd, _rows(a0), wa, w_gate_up] + [
        _rows(v) for v in (k_k, k_a, r_k.reshape(L, -1), ln_x_g, ln_x_b)]
    qg, kvg = _rows(q_norm_g), _rows(kv_norm_g)

    for l in range(L):
        za, fq, fk, fv, mq, mk, mv = _inproj(l, x, mod, g_mix_pre, w1, bf, sel, qg, wq,
                                             kvg, wkv, cos_t, sin_t, tm)
        ya = _rwkv(l, za, *rwkv_rows, min(2 * tm, S))
        yb = _attention(fq, fk, fv, HEAD64, "fox_attention")
        yc = _attention(mq, mk, mv, V_HEAD_DIM, "mla_attention")
        x = _merge(l, x, mod, g_mix_pre, g_mix_post, w_gates, ya, yb, yc, wba, wbb, wbc, wo, min(2 * tm, S))
        x = _ffn(l, x, mod, g_ffn_pre, g_ffn_post, wu, wdn, min(2 * tm, S))
    return x
```
